```python
import math
import jax, jax.numpy as jnp
from jax import lax
import numpy as np

D_MODEL = 1024
BATCH = 1
SEQ = 16384
DEPTH = 2
DEC_BATCH = 32
DEC_SEQ = 8
PAST_LEN = 16384
PAGE_SIZE = 128

D_SSM = D_MODEL // 2
GROUP_CH = 16
N_GROUPS = D_SSM // GROUP_CH
STATE_DIM = 64
N_HEADS = 4
HEAD_DIM = D_MODEL // (4 * N_HEADS)
V_DIM = 2 * HEAD_DIM
D_QK = N_HEADS * 2 * HEAD_DIM
D_ATT = N_HEADS * V_DIM
ROT_DIM = HEAD_DIM // 4
ROPE_THETA = 500000.0
Q_BLOCK = 128
DEEPNORM_ALPHA = (2 * DEPTH) ** 0.25
DEEPNORM_BETA = (8 * DEPTH) ** -0.25
LN_EPS = 1e-5

kernel_name = 'hybrid_s5_diffattn_gated_decoder_step'


def _split_points():
    sizes = [D_SSM, D_SSM, D_QK, D_QK, D_ATT, D_ATT, D_MODEL, D_MODEL]
    pts, acc = [], 0
    for s in sizes[:-1]:
        acc += s
        pts.append(acc)
    return pts


def _layer_norm(x, g, b):
    xf = x.astype(jnp.float32)
    mu = jnp.mean(xf, axis=-1, keepdims=True)
    xc = xf - mu
    var = jnp.mean(xc * xc, axis=-1, keepdims=True)
    return (xc * lax.rsqrt(var + LN_EPS) * g.astype(jnp.float32) + b.astype(jnp.float32)).astype(x.dtype)


def _rope(t, pos):
    half = ROT_DIM // 2
    inv = ROPE_THETA ** (-jnp.arange(0, ROT_DIM, 2, dtype=jnp.float32) / ROT_DIM)
    ang = pos.astype(jnp.float32)[:, None] * inv[None, :]
    cos = jnp.cos(ang)[None, :, None, None, :].astype(t.dtype)
    sin = jnp.sin(ang)[None, :, None, None, :].astype(t.dtype)
    t1 = t[..., :half]
    t2 = t[..., half:ROT_DIM]
    return jnp.concatenate([t1 * cos - t2 * sin, t2 * cos + t1 * sin, t[..., ROT_DIM:]], axis=-1)


def _s5(u, s0_re, s0_im, lam_re, lam_im, log_dt, b_re, b_im, c_re, c_im, d):
    bsz, L, _ = u.shape
    ug = u.reshape(bsz, L, N_GROUPS, GROUP_CH)
    dt = jnp.exp(log_dt)[:, None]
    mag = jnp.exp(lam_re * dt)
    ang = lam_im * dt
    ab_re = mag * jnp.cos(ang)
    ab_im = mag * jnp.sin(ang)
    den = lam_re * lam_re + lam_im * lam_im
    nr = ab_re - 1.0
    z_re = (nr * lam_re + ab_im * lam_im) / den
    z_im = (ab_im * lam_re - nr * lam_im) / den
    bb_re = z_re[..., None] * b_re - z_im[..., None] * b_im
    bb_im = z_re[..., None] * b_im + z_im[..., None] * b_re
    bu_re = jnp.einsum('blgh,gph->blgp', ug, bb_re)
    bu_im = jnp.einsum('blgh,gph->blgp', ug, bb_im)
    bu_re = bu_re.at[:, 0].add(ab_re * s0_re - ab_im * s0_im)
    bu_im = bu_im.at[:, 0].add(ab_re * s0_im + ab_im * s0_re)
    a_re = jnp.broadcast_to(ab_re, bu_re.shape)
    a_im = jnp.broadcast_to(ab_im, bu_im.shape)

    def combine(e1, e2):
        a1r, a1i, b1r, b1i = e1
        a2r, a2i, b2r, b2i = e2
        return (a1r * a2r - a1i * a2i, a1r * a2i + a1i * a2r,
                a2r * b1r - a2i * b1i + b2r, a2r * b1i + a2i * b1r + b2i)

    _, _, s_re, s_im = lax.associative_scan(combine, (a_re, a_im, bu_re, bu_im), axis=1)
    y = (jnp.einsum('blgp,ghp->blgh', s_re, c_re) - jnp.einsum('blgp,ghp->blgh', s_im, c_im)
         + d.reshape(N_GROUPS, GROUP_CH) * ug)
    return y.reshape(bsz, L, D_SSM), s_re[:, -1], s_im[:, -1]


def _diff_probs(s, lam):
    p = jax.nn.softmax(s, axis=-1)
    return p[:, :, 0] - lam * p[:, :, 1]


def _diff_attn_prompt(q, k, v, lam, scale):
    bsz, L = q.shape[0], q.shape[1]
    nb = L // Q_BLOCK
    qb = q.reshape(bsz, nb, Q_BLOCK, N_HEADS, 2, HEAD_DIM).swapaxes(0, 1)
    kpos = jnp.arange(L)

    def block(args):
        qi, i = args
        s = jnp.einsum('bqhmd,bkhmd->bhmqk', qi, k).astype(jnp.float32) * scale
        qpos = i * Q_BLOCK + jnp.arange(Q_BLOCK)
        mask = kpos[None, :] <= qpos[:, None]
        s = jnp.where(mask, s, -jnp.inf)
        p = _diff_probs(s, lam)
        return jnp.einsum('bhqk,bkhd->bqhd', p.astype(v.dtype), v)

    out = lax.map(block, (qb, jnp.arange(nb)))
    return out.swapaxes(0, 1).reshape(bsz, L, N_HEADS, V_DIM)


def _diff_attn_sample(q, k_new, v_new, k_past, v_past, lam, scale):
    T = q.shape[1]
    n_past = k_past.shape[1]
    s_past = jnp.einsum('bqhmd,bkhmd->bhmqk', q, k_past).astype(jnp.float32) * scale
    s_new = jnp.einsum('bqhmd,bkhmd->bhmqk', q, k_new).astype(jnp.float32) * scale
    tri = jnp.tril(jnp.ones((T, T), dtype=bool))
    s_new = jnp.where(tri, s_new, -jnp.inf)
    p = _diff_probs(jnp.concatenate([s_past, s_new], axis=-1), lam).astype(v_new.dtype)
    return (jnp.einsum('bhqk,bkhd->bqhd', p[..., :n_past], v_past)
            + jnp.einsum('bhqk,bkhd->bqhd', p[..., n_past:], v_new))


def _layer(x, pos, prm, layer_idx, s0_re, s0_im, past_k, past_v):
    (w_in, lam_re, lam_im, log_dt, b_re, b_im, c_re, c_im, d, w_glu, b_glu,
     lq1, lk1, lq2, lk2, subln_g, w_ps, w_pa, w_o, ln_g, ln_b) = prm
    bsz, L, _ = x.shape
    z = x @ w_in
    u, gate_s, q, k, v, gate_a, zg_s, zg_a = jnp.split(z, _split_points(), axis=-1)

    ys, sf_re, sf_im = _s5(u, s0_re, s0_im, lam_re, lam_im, log_dt, b_re, b_im, c_re, c_im, d)
    ys = jax.nn.gelu(ys)
    ys = ys * jax.nn.sigmoid(ys @ w_glu + b_glu)
    ys = ys * jax.nn.silu(gate_s)

    q = _rope(q.reshape(bsz, L, N_HEADS, 2, HEAD_DIM), pos)
    k = _rope(k.reshape(bsz, L, N_HEADS, 2, HEAD_DIM), pos)
    v = v.reshape(bsz, L, N_HEADS, V_DIM)
    lam_init = 0.8 - 0.6 * math.exp(-0.3 * layer_idx)
    lam = (jnp.exp(jnp.sum(lq1.astype(jnp.float32) * lk1.astype(jnp.float32)))
           - jnp.exp(jnp.sum(lq2.astype(jnp.float32) * lk2.astype(jnp.float32))) + lam_init)
    scale = HEAD_DIM ** -0.5
    if past_k is None:
        o = _diff_attn_prompt(q, k, v, lam, scale)
    else:
        o = _diff_attn_sample(q, k, v, past_k, past_v, lam, scale)
    of = o.astype(jnp.float32)
    of = of * lax.rsqrt(jnp.mean(of * of, axis=-1, keepdims=True) + LN_EPS)
    o = (of * subln_g.astype(jnp.float32) * (1.0 - lam_init)).astype(x.dtype)
    ya = o.reshape(bsz, L, D_ATT) * jax.nn.silu(gate_a)

    merged = jax.nn.sigmoid(zg_s) * (ys @ w_ps) + jax.nn.sigmoid(zg_a) * (ya @ w_pa)
    out = merged @ w_o
    x_new = _layer_norm(DEEPNORM_ALPHA * x + out, ln_g, ln_b)
    k_rows = k.reshape(bsz, L, N_HEADS, 2 * HEAD_DIM)
    return x_new, k_rows, v, sf_re, sf_im


def setup_inputs(seed: int = 0) -> dict:
    key = jax.random.key(seed)
    ks = jax.random.split(key, 32)
    f32 = jnp.float32

    def nrm(k, shape, s):
        return jax.random.normal(k, shape, f32) * s

    n_pages = PAST_LEN // PAGE_SIZE
    n_used = DEC_BATCH * n_pages
    n_pool = n_used + max(1, n_used // 4)
    d_in = 2 * D_SSM + 2 * D_QK + 2 * D_ATT + 2 * D_MODEL
    gp = (DEPTH, N_GROUPS, STATE_DIM)
    return {
        'x_prompt': nrm(ks[0], (BATCH, SEQ, D_MODEL), 1.0),
        'x_sample': nrm(ks[1], (DEC_BATCH, DEC_SEQ, D_MODEL), 1.0),
        'cache_k': nrm(ks[2], (DEPTH, n_pool, PAGE_SIZE, N_HEADS, 2 * HEAD_DIM), 1.0),
        'cache_v': nrm(ks[3], (DEPTH, n_pool, PAGE_SIZE, N_HEADS, V_DIM), 1.0),
        'state_ssm_re': nrm(ks[4], (DEPTH, DEC_BATCH, N_GROUPS, STATE_DIM), 0.3),
        'state_ssm_im': nrm(ks[5], (DEPTH, DEC_BATCH, N_GROUPS, STATE_DIM), 0.3),
        'page_table': jax.random.permutation(ks[6], n_pool)[:n_used].reshape(DEC_BATCH, n_pages).astype(jnp.int32),
        'w_in': nrm(ks[7], (DEPTH, D_MODEL, d_in), D_MODEL ** -0.5),
        'ssm_lambda_re': -0.5 + nrm(ks[8], gp, 0.01),
        'ssm_lambda_im': math.pi * jnp.arange(STATE_DIM, dtype=f32) + nrm(ks[9], gp, 0.01),
        'ssm_log_dt': jax.random.uniform(ks[10], (DEPTH, N_GROUPS), f32, math.log(1e-3), math.log(1e-1)),
        'ssm_b_re': nrm(ks[11], (DEPTH, N_GROUPS, STATE_DIM, GROUP_CH), (2 * GROUP_CH) ** -0.5),
        'ssm_b_im': nrm(ks[12], (DEPTH, N_GROUPS, STATE_DIM, GROUP_CH), (2 * GROUP_CH) ** -0.5),
        'ssm_c_re': nrm(ks[13], (DEPTH, N_GROUPS, GROUP_CH, STATE_DIM), STATE_DIM ** -0.5),
        'ssm_c_im': nrm(ks[14], (DEPTH, N_GROUPS, GROUP_CH, STATE_DIM), STATE_DIM ** -0.5),
        'ssm_d': nrm(ks[15], (DEPTH, D_SSM), 1.0),
        'w_glu': nrm(ks[16], (DEPTH, D_SSM, D_SSM), D_SSM ** -0.5),
        'b_glu': nrm(ks[17], (DEPTH, D_SSM), 0.01),
        'lam_q1': nrm(ks[18], (DEPTH, HEAD_DIM), 0.1),
        'lam_k1': nrm(ks[19], (DEPTH, HEAD_DIM), 0.1),
        'lam_q2': nrm(ks[20], (DEPTH, HEAD_DIM), 0.1),
        'lam_k2': nrm(ks[21], (DEPTH, HEAD_DIM), 0.1),
        'subln_g': 1.0 + nrm(ks[22], (DEPTH, V_DIM), 0.02),
        'w_proj_s': nrm(ks[23], (DEPTH, D_SSM, D_MODEL), D_SSM ** -0.5 * DEEPNORM_BETA),
        'w_proj_a': nrm(ks[24], (DEPTH, D_ATT, D_MODEL), D_ATT ** -0.5 * DEEPNORM_BETA),
        'w_out': nrm(ks[25], (DEPTH, D_MODEL, D_MODEL), D_MODEL ** -0.5 * DEEPNORM_BETA),
        'ln_g': 1.0 + nrm(ks[26], (DEPTH, D_MODEL), 0.02),
        'ln_b': nrm(ks[27], (DEPTH, D_MODEL), 0.02),
    }


def reference(x_prompt, x_sample, cache_k, cache_v, state_ssm_re, state_ssm_im, page_table,
              w_in, ssm_lambda_re, ssm_lambda_im, ssm_log_dt, ssm_b_re, ssm_b_im, ssm_c_re, ssm_c_im,
              ssm_d, w_glu, b_glu, lam_q1, lam_k1, lam_q2, lam_k2, subln_g,
              w_proj_s, w_proj_a, w_out, ln_g, ln_b):
    bsz_p, seq_p = x_prompt.shape[0], x_prompt.shape[1]
    bsz_s, seq_s = x_sample.shape[0], x_sample.shape[1]
    n_pages = page_table.shape[1]
    past_len = n_pages * PAGE_SIZE
    pos_p = jnp.arange(seq_p)
    pos_s = past_len + jnp.arange(seq_s)
    s0_p = jnp.zeros((bsz_p, N_GROUPS, STATE_DIM), x_prompt.dtype)

    xp, xs = x_prompt, x_sample
    kp_l, vp_l, srp_l, sip_l = [], [], [], []
    ks_l, vs_l, srs_l, sis_l = [], [], [], []
    for l in range(DEPTH):
        prm = (w_in[l], ssm_lambda_re[l], ssm_lambda_im[l], ssm_log_dt[l], ssm_b_re[l], ssm_b_im[l],
               ssm_c_re[l], ssm_c_im[l], ssm_d[l], w_glu[l], b_glu[l],
               lam_q1[l], lam_k1[l], lam_q2[l], lam_k2[l], subln_g[l],
               w_proj_s[l], w_proj_a[l], w_out[l], ln_g[l], ln_b[l])
        xp, kp, vp, srp, sip = _layer(xp, pos_p, prm, l, s0_p, s0_p, None, None)
        k_past = cache_k[l, page_table].reshape(bsz_s, past_len, N_HEADS, 2, HEAD_DIM)
        v_past = cache_v[l, page_table].reshape(bsz_s, past_len, N_HEADS, V_DIM)
        xs, kss, vss, srs, sis = _layer(xs, pos_s, prm, l, state_ssm_re[l], state_ssm_im[l], k_past, v_past)
        kp_l.append(kp); vp_l.append(vp); srp_l.append(srp); sip_l.append(sip)
        ks_l.append(kss); vs_l.append(vss); srs_l.append(srs); sis_l.append(sis)

    return (xp, xs,
            jnp.stack(kp_l), jnp.stack(vp_l), jnp.stack(srp_l), jnp.stack(sip_l),
            jnp.stack(ks_l), jnp.stack(vs_l), jnp.stack(srs_l), jnp.stack(sis_l))
```

```python
import functools
import math

import jax
import jax.numpy as jnp
import numpy as np
from jax import lax
from jax.experimental import pallas as pl
from jax.experimental.pallas import tpu as pltpu

F32 = jnp.float32
BF16 = jnp.bfloat16

D_MODEL = 1024
DEPTH = 2
PAGE_SIZE = 128
D_SSM = D_MODEL // 2
GROUP_CH = 16
N_GROUPS = D_SSM // GROUP_CH
STATE_DIM = 64
N_HEADS = 4
HEAD_DIM = D_MODEL // (4 * N_HEADS)
V_DIM = 2 * HEAD_DIM
D_QK = N_HEADS * 2 * HEAD_DIM
D_ATT = N_HEADS * V_DIM
ROT_DIM = HEAD_DIM // 4
ROPE_THETA = 500000.0
DEEPNORM_ALPHA = (2 * DEPTH) ** 0.25
LN_EPS = 1e-5
D_IN = 2 * D_SSM + 2 * D_QK + 2 * D_ATT + 2 * D_MODEL

LANES = 128
SUBLANES = 8

CHUNK = SUBLANES
GROUPS_PER_BLOCK = LANES // GROUP_CH
N_CH_BLOCKS = D_SSM // LANES
BLOCK_STATE = GROUPS_PER_BLOCK * STATE_DIM
BLOCK_IO = CHUNK * LANES

NEG_BIG = -1e30
VMEM_LIMIT = 56 * 1024 * 1024

_OFF_U = 0
_OFF_GS = _OFF_U + D_SSM
_OFF_Q = _OFF_GS + D_SSM
_OFF_K = _OFF_Q + D_QK
_OFF_V = _OFF_K + D_QK
_OFF_GA = _OFF_V + D_ATT
_OFF_ZS = _OFF_GA + D_ATT
_OFF_ZA = _OFF_ZS + D_MODEL
N_GATES = 2 * D_SSM + 2 * D_MODEL


def _cparams(semantics):
    return pltpu.CompilerParams(dimension_semantics=semantics, vmem_limit_bytes=VMEM_LIMIT)


def _inproj_kernel(x_ref, w_ref, cos_ref, sa_ref, sb_ref,
                   u_ref, gates_ref, q_ref, kf_ref, kb_ref, vf_ref, vb_ref):
    xb = x_ref[...].astype(BF16)

    def seg(off, width):
        return jnp.dot(xb, w_ref[:, off:off + width], preferred_element_type=F32)

    u = seg(_OFF_U, D_SSM)
    for cb in range(N_CH_BLOCKS):
        u_ref[cb] = u[:, cb * LANES:(cb + 1) * LANES]
    gates_ref[:, 0:D_SSM] = seg(_OFF_GS, D_SSM)
    gates_ref[:, D_SSM:2 * D_SSM] = seg(_OFF_GA, D_ATT)
    gates_ref[:, 2 * D_SSM:2 * D_SSM + D_MODEL] = seg(_OFF_ZS, D_MODEL)
    gates_ref[:, 2 * D_SSM + D_MODEL:] = seg(_OFF_ZA, D_MODEL)

    reps = D_QK // LANES
    cos = jnp.concatenate([cos_ref[...]] * reps, axis=1)
    sa = jnp.concatenate([sa_ref[...]] * reps, axis=1)
    sb = jnp.concatenate([sb_ref[...]] * reps, axis=1)
    half = ROT_DIM // 2

    def rope(t):
        return (t * cos + pltpu.roll(t, D_QK - half, 1) * sa + pltpu.roll(t, half, 1) * sb)

    q = rope(seg(_OFF_Q, D_QK))
    q_ref[...] = (q * (HEAD_DIM ** -0.5)).astype(BF16)
    k = rope(seg(_OFF_K, D_QK))
    kf_ref[...] = k
    kb_ref[...] = k.astype(BF16)
    v = seg(_OFF_V, D_ATT)
    vf_ref[...] = v
    vb_ref[...] = v.astype(BF16)


def _in_projection(x, w_bf, cos_t, sa_t, sb_t, tm):
    m = x.shape[0]
    row = lambda i: (i, 0)
    return pl.pallas_call(
        _inproj_kernel,
        grid=(m // tm,),
        in_specs=[
            pl.BlockSpec((tm, D_MODEL), row),
            pl.BlockSpec((D_MODEL, D_IN), lambda i: (0, 0), pipeline_mode=pl.Buffered(1)),
            pl.BlockSpec((tm, LANES), row),
            pl.BlockSpec((tm, LANES), row),
            pl.BlockSpec((tm, LANES), row),
        ],
        out_specs=[
            pl.BlockSpec((N_CH_BLOCKS, tm, LANES), lambda i: (0, i, 0)),
            pl.BlockSpec((tm, N_GATES), row),
            pl.BlockSpec((tm, D_QK), row),
            pl.BlockSpec((tm, D_QK), row),
            pl.BlockSpec((tm, D_QK), row),
            pl.BlockSpec((tm, D_ATT), row),
            pl.BlockSpec((tm, D_ATT), row),
        ],
        out_shape=[
            jax.ShapeDtypeStruct((N_CH_BLOCKS, m, LANES), F32),
            jax.ShapeDtypeStruct((m, N_GATES), F32),
            jax.ShapeDtypeStruct((m, D_QK), BF16),
            jax.ShapeDtypeStruct((m, D_QK), F32),
            jax.ShapeDtypeStruct((m, D_QK), BF16),
            jax.ShapeDtypeStruct((m, D_ATT), F32),
            jax.ShapeDtypeStruct((m, D_ATT), BF16),
        ],
        compiler_params=_cparams(("arbitrary",)),
        name="in_projection",
    )(x, w_bf, cos_t, sa_t, sb_t)


def _s5_kernel(x_ref, w_in_ref, w_st_ref, w_loc_ref, a_ref, d_ref, s0_ref,
               y_ref, sfin_ref, *scratch, independent_rows):
    x = x_ref[0]
    xb = x.astype(BF16)
    p = jnp.dot(xb, w_in_ref[0], preferred_element_type=F32)
    a_re = a_ref[0][:, :BLOCK_STATE]
    a_im = a_ref[0][:, BLOCK_STATE:]

    def step(sr, si, pr, pi):
        return a_re * sr - a_im * si + pr, a_re * si + a_im * sr + pi

    if independent_rows:
        s_prev = s0_ref[0]
        nr, ni = step(s_prev[:, :BLOCK_STATE], s_prev[:, BLOCK_STATE:],
                      p[:, :BLOCK_STATE], p[:, BLOCK_STATE:])
        sfin_ref[0] = jnp.concatenate([nr, ni], axis=1)
    else:
        state_sc, p_sc, prev_sc = scratch
        rows = x.shape[0]

        @pl.when(pl.program_id(1) == 0)
        def _():
            state_sc[...] = s0_ref[0]

        p_sc[...] = p

        def body(r, carry):
            sr, si = carry
            prev_sc[pl.ds(r, 1), :BLOCK_STATE] = sr
            prev_sc[pl.ds(r, 1), BLOCK_STATE:] = si
            return step(sr, si, p_sc[pl.ds(r, 1), :BLOCK_STATE], p_sc[pl.ds(r, 1), BLOCK_STATE:])

        sr, si = lax.fori_loop(0, rows, body,
                               (state_sc[:, :BLOCK_STATE], state_sc[:, BLOCK_STATE:]))
        state_sc[:, :BLOCK_STATE] = sr
        state_sc[:, BLOCK_STATE:] = si
        sfin_ref[0] = state_sc[...]
        s_prev = prev_sc[...]

    y = jnp.dot(s_prev.astype(BF16), w_st_ref[0], preferred_element_type=F32)
    y = y + jnp.dot(xb, w_loc_ref[0], preferred_element_type=F32)
    y_ref[0] = y + d_ref[0] * x


def _s5_branch(u_blk, w_in, w_st, w_loc, a_chunk, d_blk, s0, independent_rows, rows):
    n_chunks = u_blk.shape[1]
    n_seq = s0.shape[1]
    wspec = pl.BlockSpec((1, BLOCK_IO, BLOCK_IO), lambda c, t: (c, 0, 0))
    vspec = lambda w: pl.BlockSpec((1, 1, w), lambda c, t: (c, 0, 0))
    scratch = [] if independent_rows else [
        pltpu.VMEM((1, 2 * BLOCK_STATE), F32),
        pltpu.VMEM((rows, 2 * BLOCK_STATE), F32),
        pltpu.VMEM((rows, 2 * BLOCK_STATE), F32),
    ]
    return pl.pallas_call(
        functools.partial(_s5_kernel, independent_rows=independent_rows),
        grid=(N_CH_BLOCKS, n_chunks // rows),
        in_specs=[
            pl.BlockSpec((1, rows, BLOCK_IO), lambda c, t: (c, t, 0)),
            wspec, wspec, wspec,
            vspec(2 * BLOCK_STATE), vspec(BLOCK_IO),
            pl.BlockSpec((1, n_seq, 2 * BLOCK_STATE), lambda c, t: (c, 0, 0)),
        ],
        out_specs=[
            pl.BlockSpec((1, rows, BLOCK_IO), lambda c, t: (c, t, 0)),
            pl.BlockSpec((1, n_seq, 2 * BLOCK_STATE), lambda c, t: (c, 0, 0)),
        ],
        out_shape=[
            jax.ShapeDtypeStruct(u_blk.shape, F32),
            jax.ShapeDtypeStruct(s0.shape, F32),
        ],
        scratch_shapes=scratch,
        compiler_params=_cparams(("arbitrary", "arbitrary")),
        name="s5_branch",
    )(u_blk, w_in, w_st, w_loc, a_chunk, d_blk, s0)


def _s5_operators(lam_re, lam_im, log_dt, b_re, b_im, c_re, c_im, d):
    g, gp, nb = N_GROUPS, GROUPS_PER_BLOCK, N_CH_BLOCKS
    dt = jnp.exp(log_dt)[:, None]
    kk = jnp.arange(CHUNK + 1, dtype=F32)[:, None, None]
    mag = jnp.exp(kk * (lam_re * dt)[None])
    ang = kk * (lam_im * dt)[None]
    ak_re, ak_im = mag * jnp.cos(ang), mag * jnp.sin(ang)
    ab_re, ab_im = ak_re[1], ak_im[1]
    den = lam_re * lam_re + lam_im * lam_im
    nr = ab_re - 1.0
    z_re = (nr * lam_re + ab_im * lam_im) / den
    z_im = (ab_im * lam_re - nr * lam_im) / den
    bb_re = z_re[..., None] * b_re - z_im[..., None] * b_im
    bb_im = z_re[..., None] * b_im + z_im[..., None] * b_re
    eye = jnp.eye(gp, dtype=F32)

    kr = (CHUNK - 1.0) - kk[:CHUNK]
    mag_r = jnp.exp(kr * (lam_re * dt)[None])
    ang_r = kr * (lam_im * dt)[None]
    pw_re = (mag_r * jnp.cos(ang_r))[..., None]
    pw_im = (mag_r * jnp.sin(ang_r))[..., None]
    m_re = pw_re * bb_re[None] - pw_im * bb_im[None]
    m_im = pw_re * bb_im[None] + pw_im * bb_re[None]
    m = jnp.stack([m_re, m_im], axis=1)
    m = m.reshape(CHUNK, 2, nb, gp, STATE_DIM, GROUP_CH).transpose(2, 0, 3, 5, 1, 4)
    w_in = (m[:, :, :, :, :, None, :] * eye[None, None, :, None, None, :, None])
    w_in = w_in.reshape(nb, BLOCK_IO, 2 * BLOCK_STATE)

    e_re = c_re[None] * ak_re[1:, :, None, :] - c_im[None] * ak_im[1:, :, None, :]
    e_im = c_re[None] * ak_im[1:, :, None, :] + c_im[None] * ak_re[1:, :, None, :]
    e = jnp.stack([e_re, -e_im], axis=0)
    e = e.reshape(2, CHUNK, nb, gp, GROUP_CH, STATE_DIM).transpose(2, 0, 3, 5, 1, 4)
    w_st = (e[:, :, :, :, :, None, :] * eye[None, None, :, None, None, :, None])
    w_st = w_st.reshape(nb, 2 * BLOCK_STATE, BLOCK_IO)

    ca_re = c_re[None] * ak_re[:CHUNK, :, None, :] - c_im[None] * ak_im[:CHUNK, :, None, :]
    ca_im = c_re[None] * ak_im[:CHUNK, :, None, :] + c_im[None] * ak_re[:CHUNK, :, None, :]
    bt_re = bb_re.transpose(0, 2, 1)[None, :, None]
    bt_im = bb_im.transpose(0, 2, 1)[None, :, None]
    kern = jnp.sum(ca_re[:, :, :, None, :] * bt_re - ca_im[:, :, :, None, :] * bt_im, axis=-1)
    lag = np.arange(CHUNK)[None, :] - np.arange(CHUNK)[:, None]
    toe = jnp.where((lag >= 0)[:, :, None, None, None], kern[np.clip(lag, 0, None)], 0.0)
    toe = toe.reshape(CHUNK, CHUNK, nb, gp, GROUP_CH, GROUP_CH).transpose(2, 0, 3, 5, 1, 4)
    w_loc = (toe[:, :, :, :, :, None, :] * eye[None, None, :, None, None, :, None])
    w_loc = w_loc.reshape(nb, BLOCK_IO, BLOCK_IO)

    a_chunk = jnp.stack([ak_re[CHUNK], ak_im[CHUNK]], axis=0)
    a_chunk = a_chunk.reshape(2, nb, BLOCK_STATE).transpose(1, 0, 2).reshape(nb, 1, 2 * BLOCK_STATE)
    d_blk = jnp.tile(d.reshape(nb, 1, LANES), (1, 1, CHUNK))
    return w_in.astype(BF16), w_st.astype(BF16), w_loc.astype(BF16), a_chunk, d_blk


def _state_to_blocks(s_re, s_im):
    n = s_re.shape[0]
    s = jnp.stack([s_re, s_im], axis=1).reshape(n, 2, N_CH_BLOCKS, BLOCK_STATE)
    return s.transpose(2, 0, 1, 3).reshape(N_CH_BLOCKS, n, 2 * BLOCK_STATE)


def _blocks_to_state(s):
    n = s.shape[1]
    s = s.reshape(N_CH_BLOCKS, n, 2, BLOCK_STATE).transpose(2, 1, 0, 3)
    s = s.reshape(2, n, N_GROUPS, STATE_DIM)
    return s[0], s[1]


def _lambda_value(lam_ref, lam_init):
    lv = lam_ref[...]
    t1 = jnp.sum(lv[0:1] * lv[1:2], axis=1, keepdims=True)
    t2 = jnp.sum(lv[2:3] * lv[3:4], axis=1, keepdims=True)
    return jnp.exp(t1) - jnp.exp(t2) + lam_init


def _split_subheads(q):
    lane = lax.broadcasted_iota(jnp.int32, q.shape, 1)
    zero = jnp.zeros_like(q)
    return jnp.concatenate([jnp.where(lane < HEAD_DIM, q, zero),
                            jnp.where(lane >= HEAD_DIM, q, zero)], axis=0)


def _finish_heads(o1, l1, o2, l2, lam, g, lam_init):
    of = o1 / l1 - lam * (o2 / l2)
    of = of * lax.rsqrt(jnp.mean(of * of, axis=-1, keepdims=True) + LN_EPS)
    return of * g * (1.0 - lam_init)


def _attn_prompt_kernel(lam_ref, g_ref, q_ref, k_ref, v_ref, o_ref, m_sc, acc_sc, *, tile, lam_init):
    i = pl.program_id(1)
    qs = _split_subheads(q_ref[...])
    m_sc[...] = jnp.full(m_sc.shape, NEG_BIG, F32)
    acc_sc[...] = jnp.zeros(acc_sc.shape, F32)
    ones = jnp.ones((tile, LANES), BF16)

    def block(j, masked):
        kj = k_ref[pl.ds(j * tile, tile), :]
        vj = jnp.concatenate([v_ref[pl.ds(j * tile, tile), :], ones], axis=1)
        s = lax.dot_general(qs, kj, (((1,), (1,)), ((), ())), preferred_element_type=F32)
        if masked:
            row = lax.broadcasted_iota(jnp.int32, s.shape, 0)
            col = lax.broadcasted_iota(jnp.int32, s.shape, 1)
            qpos = jnp.where(row >= tile, row - tile, row)
            s = jnp.where(col <= qpos, s, NEG_BIG)
        m_old = m_sc[...]
        m_new = jnp.maximum(m_old, jnp.max(s, axis=1, keepdims=True))
        p = jnp.exp(s - m_new)
        acc_sc[...] = (jnp.exp(m_old - m_new) * acc_sc[...]
                       + jnp.dot(p.astype(BF16), vj, preferred_element_type=F32))
        m_sc[...] = m_new

    def body(j, c):
        block(j, False)
        return c

    lax.fori_loop(0, i, body, 0)
    block(i, True)

    acc = acc_sc[...]
    lam = _lambda_value(lam_ref, lam_init)
    o_ref[...] = _finish_heads(acc[:tile, :V_DIM], acc[:tile, V_DIM:V_DIM + 1],
                               acc[tile:, :V_DIM], acc[tile:, V_DIM:V_DIM + 1],
                               lam, g_ref[...], lam_init)


def _attention_prompt(q_bf, k_bf, v_bf, lam_vec, g, lam_init, tile):
    seq = q_bf.shape[0]
    head_all = pl.BlockSpec((seq, 2 * HEAD_DIM), lambda h, i: (0, h))
    return pl.pallas_call(
        functools.partial(_attn_prompt_kernel, tile=tile, lam_init=lam_init),
        grid=(N_HEADS, seq // tile),
        in_specs=[
            pl.BlockSpec((4, HEAD_DIM), lambda h, i: (0, 0)),
            pl.BlockSpec((1, V_DIM), lambda h, i: (0, 0)),
            pl.BlockSpec((tile, 2 * HEAD_DIM), lambda h, i: (i, h)),
            head_all, head_all,
        ],
        out_specs=pl.BlockSpec((tile, V_DIM), lambda h, i: (i, h)),
        out_shape=jax.ShapeDtypeStruct((seq, D_ATT), F32),
        scratch_shapes=[pltpu.VMEM((2 * tile, 1), F32), pltpu.VMEM((2 * tile, 2 * V_DIM), F32)],
        compiler_params=_cparams(("arbitrary", "arbitrary")),
        name="attention_prompt",
    )(lam_vec, g, q_bf, k_bf, v_bf)


def _attn_sample_kernel(pt_ref, lam_ref, g_ref, q_ref, kn_ref, vn_ref, *rest,
                        pages, n_new, lam_init):
    del pt_ref
    k_refs = rest[:pages]
    v_refs = rest[pages:2 * pages]
    o_ref, m_sc, l_sc, acc_sc = rest[2 * pages:]
    c = pl.program_id(1)
    rows = 2 * n_new

    @pl.when(c == 0)
    def _():
        m_sc[...] = jnp.full(m_sc.shape, NEG_BIG, F32)
        l_sc[...] = jnp.zeros(l_sc.shape, F32)
        acc_sc[...] = jnp.zeros(acc_sc.shape, F32)

    def update(h, s, vals):
        m_old = m_sc[h]
        m_new = jnp.maximum(m_old, jnp.max(s, axis=1, keepdims=True))
        p = jnp.exp(s - m_new)
        scale = jnp.exp(m_old - m_new)
        l_sc[h] = scale * l_sc[h] + jnp.sum(p, axis=1, keepdims=True)
        acc_sc[h] = scale * acc_sc[h] + jnp.dot(p.astype(BF16), vals, preferred_element_type=F32)
        m_sc[h] = m_new

    def scores(qs, keys):
        return lax.dot_general(qs, keys, (((1,), (1,)), ((), ())), preferred_element_type=F32)

    q = q_ref[0].astype(F32)
    for h in range(N_HEADS):
        hs = slice(h * V_DIM, (h + 1) * V_DIM)
        qs = _split_subheads(q[:, hs]).astype(BF16)
        keys = jnp.concatenate([r[:, hs] for r in k_refs], axis=0).astype(BF16)
        vals = jnp.concatenate([r[:, hs] for r in v_refs], axis=0).astype(BF16)
        update(h, scores(qs, keys), vals)

    @pl.when(c == pl.num_programs(1) - 1)
    def _():
        lam = _lambda_value(lam_ref, lam_init)
        kn = kn_ref[0]
        vn = vn_ref[0]
        pad = jnp.zeros((LANES - n_new, V_DIM), F32)
        for h in range(N_HEADS):
            hs = slice(h * V_DIM, (h + 1) * V_DIM)
            qs = _split_subheads(q[:, hs]).astype(BF16)
            keys = jnp.concatenate([kn[:, hs], pad], axis=0).astype(BF16)
            vals = jnp.concatenate([vn[:, hs], pad], axis=0).astype(BF16)
            s = scores(qs, keys)
            row = lax.broadcasted_iota(jnp.int32, s.shape, 0)
            col = lax.broadcasted_iota(jnp.int32, s.shape, 1)
            qpos = jnp.where(row >= n_new, row - n_new, row)
            update(h, jnp.where(col <= qpos, s, NEG_BIG), vals)
            acc = acc_sc[h]
            l = l_sc[h]
            o_ref[0, :, hs] = _finish_heads(acc[:n_new], l[:n_new], acc[n_new:], l[n_new:],
                                            lam, g_ref[...], lam_init)


def _attention_sample(page_table, q_bf, k_new, v_new, cache_k, cache_v, layer, lam_vec, g,
                      lam_init, pages):
    batch, n_new, _ = q_bf.shape
    n_pages = page_table.shape[1]
    per_b = pl.BlockSpec((1, n_new, D_QK), lambda b, c, pt: (b, 0, 0))

    def page_spec(i):
        return pl.BlockSpec((None, None, PAGE_SIZE, D_QK),
                            lambda b, c, pt: (layer, pt[b, c * pages + i], 0, 0))

    grid_spec = pltpu.PrefetchScalarGridSpec(
        num_scalar_prefetch=1,
        grid=(batch, n_pages // pages),
        in_specs=[
            pl.BlockSpec((4, HEAD_DIM), lambda b, c, pt: (0, 0)),
            pl.BlockSpec((1, V_DIM), lambda b, c, pt: (0, 0)),
            per_b, per_b, per_b,
        ] + [page_spec(i) for i in range(pages)] * 2,
        out_specs=per_b,
        scratch_shapes=[
            pltpu.VMEM((N_HEADS, 2 * n_new, 1), F32),
            pltpu.VMEM((N_HEADS, 2 * n_new, 1), F32),
            pltpu.VMEM((N_HEADS, 2 * n_new, V_DIM), F32),
        ],
    )
    return pl.pallas_call(
        functools.partial(_attn_sample_kernel, pages=pages, n_new=n_new, lam_init=lam_init),
        grid_spec=grid_spec,
        out_shape=jax.ShapeDtypeStruct((batch, n_new, D_ATT), F32),
        compiler_params=_cparams(("arbitrary", "arbitrary")),
        name="attention_sample",
    )(page_table, lam_vec, g, q_bf, k_new, v_new, *([cache_k] * pages), *([cache_v] * pages))


def _sigmoid(x):
    return 1.0 / (1.0 + jnp.exp(-x))


def _gelu_tanh(x):
    return 0.5 * x * (1.0 + jnp.tanh(math.sqrt(2.0 / math.pi) * (x + 0.044715 * (x * x * x))))


def _tail_kernel(x_ref, y_ref, gates_ref, o_ref, wg_ref, bg_ref, wps_ref, wpa_ref, wo_ref,
                 lng_ref, lnb_ref, out_ref):
    ys = jnp.concatenate([y_ref[cb] for cb in range(N_CH_BLOCKS)], axis=1)
    ys = _gelu_tanh(ys)
    glu = jnp.dot(ys.astype(BF16), wg_ref[...], preferred_element_type=F32) + bg_ref[...]
    ys = ys * _sigmoid(glu)
    gate_s = gates_ref[:, 0:D_SSM]
    ys = ys * (gate_s * _sigmoid(gate_s))
    gate_a = gates_ref[:, D_SSM:2 * D_SSM]
    ya = o_ref[...] * (gate_a * _sigmoid(gate_a))
    zg_s = gates_ref[:, 2 * D_SSM:2 * D_SSM + D_MODEL]
    zg_a = gates_ref[:, 2 * D_SSM + D_MODEL:]
    merged = (_sigmoid(zg_s) * jnp.dot(ys.astype(BF16), wps_ref[...], preferred_element_type=F32)
              + _sigmoid(zg_a) * jnp.dot(ya.astype(BF16), wpa_ref[...], preferred_element_type=F32))
    out = jnp.dot(merged.astype(BF16), wo_ref[...], preferred_element_type=F32)
    r = DEEPNORM_ALPHA * x_ref[...] + out
    mu = jnp.mean(r, axis=-1, keepdims=True)
    rc = r - mu
    var = jnp.mean(rc * rc, axis=-1, keepdims=True)
    out_ref[...] = rc * lax.rsqrt(var + LN_EPS) * lng_ref[...] + lnb_ref[...]


def _tail(x, y_blk, gates, o, w_glu, b_glu, w_ps, w_pa, w_o, ln_g, ln_b, tm):
    m = x.shape[0]
    row = lambda i: (i, 0)
    const = lambda shape: pl.BlockSpec(shape, lambda i: (0, 0), pipeline_mode=pl.Buffered(1))
    return pl.pallas_call(
        _tail_kernel,
        grid=(m // tm,),
        in_specs=[
            pl.BlockSpec((tm, D_MODEL), row),
            pl.BlockSpec((N_CH_BLOCKS, tm, LANES), lambda i: (0, i, 0)),
            pl.BlockSpec((tm, N_GATES), row),
            pl.BlockSpec((tm, D_ATT), row),
            const((D_SSM, D_SSM)), const((1, D_SSM)),
            const((D_SSM, D_MODEL)), const((D_ATT, D_MODEL)), const((D_MODEL, D_MODEL)),
            const((1, D_MODEL)), const((1, D_MODEL)),
        ],
        out_specs=pl.BlockSpec((tm, D_MODEL), row),
        out_shape=jax.ShapeDtypeStruct((m, D_MODEL), F32),
        compiler_params=_cparams(("arbitrary",)),
        name="merge_tail",
    )(x, y_blk, gates, o, w_glu, b_glu, w_ps, w_pa, w_o, ln_g, ln_b)


def _rope_tables(pos):
    half = ROT_DIM // 2
    inv = ROPE_THETA ** (-jnp.arange(0, ROT_DIM, 2, dtype=F32) / ROT_DIM)
    ang = pos.astype(F32)[:, None] * inv[None, :]
    cos, sin = jnp.cos(ang), jnp.sin(ang)
    n = pos.shape[0]
    pad = HEAD_DIM - ROT_DIM
    c = jnp.concatenate([cos, cos, jnp.ones((n, pad), F32)], axis=1)
    sa = jnp.concatenate([-sin, jnp.zeros((n, half + pad), F32)], axis=1)
    sb = jnp.concatenate([jnp.zeros((n, half), F32), sin, jnp.zeros((n, pad), F32)], axis=1)
    rep = LANES // HEAD_DIM
    return jnp.tile(c, (1, rep)), jnp.tile(sa, (1, rep)), jnp.tile(sb, (1, rep))


def kernel(x_prompt, x_sample, cache_k, cache_v, state_ssm_re, state_ssm_im, page_table, w_in, ssm_lambda_re, ssm_lambda_im, ssm_log_dt, ssm_b_re, ssm_b_im, ssm_c_re, ssm_c_im, ssm_d, w_glu, b_glu, lam_q1, lam_k1, lam_q2, lam_k2, subln_g, w_proj_s, w_proj_a, w_out, ln_g, ln_b):
    bsz_p, seq_p, _ = x_prompt.shape
    bsz_s, seq_s, _ = x_sample.shape
    assert bsz_p == 1 and seq_s == CHUNK
    n_pages = page_table.shape[1]
    past_len = n_pages * PAGE_SIZE
    n_pool = cache_k.shape[1]
    m_p, m_s = bsz_p * seq_p, bsz_s * seq_s

    rope_p = _rope_tables(jnp.arange(seq_p))
    rope_s = tuple(jnp.tile(t, (bsz_s, 1)) for t in _rope_tables(past_len + jnp.arange(seq_s)))
    ck = cache_k.reshape(DEPTH, n_pool, PAGE_SIZE, D_QK)
    cv = cache_v.reshape(DEPTH, n_pool, PAGE_SIZE, D_ATT)

    xp = x_prompt.reshape(m_p, D_MODEL)
    xs = x_sample.reshape(m_s, D_MODEL)
    zero_state = jnp.zeros((N_CH_BLOCKS, bsz_p, 2 * BLOCK_STATE), F32)
    outs = [[] for _ in range(8)]
    for l in range(DEPTH):
        lam_init = 0.8 - 0.6 * math.exp(-0.3 * l)
        w_bf = w_in[l].astype(BF16)
        ops = _s5_operators(ssm_lambda_re[l], ssm_lambda_im[l], ssm_log_dt[l], ssm_b_re[l],
                            ssm_b_im[l], ssm_c_re[l], ssm_c_im[l], ssm_d[l])
        lam_vec = jnp.stack([lam_q1[l], lam_k1[l], lam_q2[l], lam_k2[l]], axis=0)
        g = subln_g[l].reshape(1, V_DIM)
        tail_w = (w_glu[l].astype(BF16), b_glu[l].reshape(1, D_SSM), w_proj_s[l].astype(BF16),
                  w_proj_a[l].astype(BF16), w_out[l].astype(BF16),
                  ln_g[l].reshape(1, D_MODEL), ln_b[l].reshape(1, D_MODEL))

        u, gates, q_bf, k_f, k_bf, v_f, v_bf = _in_projection(xp, w_bf, *rope_p, tm=512)
        y, sfin = _s5_branch(u.reshape(N_CH_BLOCKS, m_p // CHUNK, BLOCK_IO), *ops, zero_state,
                             independent_rows=False, rows=256)
        o = _attention_prompt(q_bf, k_bf, v_bf, lam_vec, g, lam_init, tile=512)
        xp = _tail(xp, y.reshape(N_CH_BLOCKS, m_p, LANES), gates, o, *tail_w, tm=512)
        sr, si = _blocks_to_state(sfin)
        outs[0].append(k_f.reshape(bsz_p, seq_p, N_HEADS, 2 * HEAD_DIM))
        outs[1].append(v_f.reshape(bsz_p, seq_p, N_HEADS, V_DIM))
        outs[2].append(sr)
        outs[3].append(si)

        u, gates, q_bf, k_f, k_bf, v_f, v_bf = _in_projection(xs, w_bf, *rope_s, tm=m_s)
        s0 = _state_to_blocks(state_ssm_re[l], state_ssm_im[l])
        y, sfin = _s5_branch(u.reshape(N_CH_BLOCKS, bsz_s, BLOCK_IO), *ops, s0,
                             independent_rows=True, rows=bsz_s)
        o = _attention_sample(page_table, q_bf.reshape(bsz_s, seq_s, D_QK),
                              k_f.reshape(bsz_s, seq_s, D_QK), v_f.reshape(bsz_s, seq_s, D_ATT),
                              ck, cv, l, lam_vec, g, lam_init, pages=16)
        xs = _tail(xs, y.reshape(N_CH_BLOCKS, m_s, LANES), gates, o.reshape(m_s, D_ATT),
                   *tail_w, tm=m_s)
        sr, si = _blocks_to_state(sfin)
        outs[4].append(k_f.reshape(bsz_s, seq_s, N_HEADS, 2 * HEAD_DIM))
        outs[5].append(v_f.reshape(bsz_s, seq_s, N_HEADS, V_DIM))
        outs[6].append(sr)
        outs[7].append(si)

    return (xp.reshape(bsz_p, seq_p, D_MODEL), xs.reshape(bsz_s, seq_s, D_MODEL),
            *(jnp.stack(o) for o in outs))
```

```python
import functools
import math

import jax
import jax.numpy as jnp
import numpy as np
from jax import lax
from jax.experimental import pallas as pl
from jax.experimental.pallas import tpu as pltpu

F32 = jnp.float32
BF16 = jnp.bfloat16

D_MODEL = 1024
DEPTH = 2
PAGE_SIZE = 128
D_SSM = D_MODEL // 2
GROUP_CH = 16
N_GROUPS = D_SSM // GROUP_CH
STATE_DIM = 64
N_HEADS = 4
HEAD_DIM = D_MODEL // (4 * N_HEADS)
V_DIM = 2 * HEAD_DIM
D_QK = N_HEADS * 2 * HEAD_DIM
D_ATT = N_HEADS * V_DIM
ROT_DIM = HEAD_DIM // 4
ROPE_THETA = 500000.0
DEEPNORM_ALPHA = (2 * DEPTH) ** 0.25
LN_EPS = 1e-5
D_IN = 2 * D_SSM + 2 * D_QK + 2 * D_ATT + 2 * D_MODEL

LANES = 128
SUBLANES = 8

CHUNK = SUBLANES
GROUPS_PER_BLOCK = LANES // GROUP_CH
N_CH_BLOCKS = D_SSM // LANES
BLOCK_STATE = GROUPS_PER_BLOCK * STATE_DIM
BLOCK_IO = CHUNK * LANES

NEG_BIG = -1e30
LOG2_E = 1.4426950408889634
VMEM_LIMIT = 56 * 1024 * 1024

_OFF_U = 0
_OFF_GS = _OFF_U + D_SSM
_OFF_Q = _OFF_GS + D_SSM
_OFF_K = _OFF_Q + D_QK
_OFF_V = _OFF_K + D_QK
_OFF_GA = _OFF_V + D_ATT
_OFF_ZS = _OFF_GA + D_ATT
_OFF_ZA = _OFF_ZS + D_MODEL
N_GATES = 2 * D_SSM + 2 * D_MODEL


def _cparams(semantics):
    return pltpu.CompilerParams(dimension_semantics=semantics, vmem_limit_bytes=VMEM_LIMIT)


def _inproj_kernel(x_ref, w_ref, cos_ref, sa_ref, sb_ref, kf_in, vf_in,
                   u_ref, gates_ref, q_ref, kf_ref, kb_ref, vf_ref, vb_ref):
    del kf_in, vf_in
    tm = x_ref.shape[0]
    xb = x_ref[...].astype(BF16)

    def seg(off, width):
        return jnp.dot(xb, w_ref[:, off:off + width], preferred_element_type=F32)

    u = seg(_OFF_U, D_SSM)
    for cb in range(N_CH_BLOCKS):
        u_ref[cb] = u[:, cb * LANES:(cb + 1) * LANES]
    gates_ref[:, 0:D_SSM] = seg(_OFF_GS, D_SSM)
    gates_ref[:, D_SSM:2 * D_SSM] = seg(_OFF_GA, D_ATT)
    gates_ref[:, 2 * D_SSM:2 * D_SSM + D_MODEL] = seg(_OFF_ZS, D_MODEL)
    gates_ref[:, 2 * D_SSM + D_MODEL:] = seg(_OFF_ZA, D_MODEL)

    reps = D_QK // LANES
    cos = jnp.concatenate([cos_ref[...]] * reps, axis=1)
    sa = jnp.concatenate([sa_ref[...]] * reps, axis=1)
    sb = jnp.concatenate([sb_ref[...]] * reps, axis=1)
    half = ROT_DIM // 2

    def rope(t):
        return (t * cos + pltpu.roll(t, D_QK - half, 1) * sa + pltpu.roll(t, half, 1) * sb)

    q = rope(seg(_OFF_Q, D_QK))
    q_ref[...] = (q * (HEAD_DIM ** -0.5 * LOG2_E)).astype(BF16)
    k = rope(seg(_OFF_K, D_QK))
    kb_ref[...] = k.astype(BF16)
    v = seg(_OFF_V, D_ATT)
    vb_ref[...] = v.astype(BF16)
    for h in range(N_HEADS):
        kf_ref[pl.ds(h, tm, stride=N_HEADS), :] = k[:, h * V_DIM:(h + 1) * V_DIM]
        vf_ref[pl.ds(h, tm, stride=N_HEADS), :] = v[:, h * V_DIM:(h + 1) * V_DIM]


def _in_projection(x, w_bf, cos_t, sa_t, sb_t, kf_all, vf_all, layer, tm):
    m = x.shape[0]
    row = lambda i: (i, 0)
    kv_rows = pl.BlockSpec((None, tm * N_HEADS, V_DIM), lambda i: (layer, i, 0))
    return pl.pallas_call(
        _inproj_kernel,
        grid=(m // tm,),
        in_specs=[
            pl.BlockSpec((tm, D_MODEL), row),
            pl.BlockSpec((D_MODEL, D_IN), lambda i: (0, 0), pipeline_mode=pl.Buffered(1)),
            pl.BlockSpec((tm, LANES), row),
            pl.BlockSpec((tm, LANES), row),
            pl.BlockSpec((tm, LANES), row),
            pl.BlockSpec(memory_space=pl.ANY),
            pl.BlockSpec(memory_space=pl.ANY),
        ],
        out_specs=[
            pl.BlockSpec((N_CH_BLOCKS, tm, LANES), lambda i: (0, i, 0)),
            pl.BlockSpec((tm, N_GATES), row),
            pl.BlockSpec((tm, D_QK), row),
            kv_rows,
            pl.BlockSpec((tm, D_QK), row),
            kv_rows,
            pl.BlockSpec((tm, D_ATT), row),
        ],
        out_shape=[
            jax.ShapeDtypeStruct((N_CH_BLOCKS, m, LANES), F32),
            jax.ShapeDtypeStruct((m, N_GATES), F32),
            jax.ShapeDtypeStruct((m, D_QK), BF16),
            jax.ShapeDtypeStruct(kf_all.shape, F32),
            jax.ShapeDtypeStruct((m, D_QK), BF16),
            jax.ShapeDtypeStruct(vf_all.shape, F32),
            jax.ShapeDtypeStruct((m, D_ATT), BF16),
        ],
        input_output_aliases={5: 3, 6: 5},
        compiler_params=_cparams(("arbitrary",)),
        name="in_projection",
    )(x, w_bf, cos_t, sa_t, sb_t, kf_all, vf_all)


def _s5_kernel(x_ref, w_in_ref, w_st_ref, w_loc_ref, a_ref, d_ref, s0_ref,
               y_ref, sfin_ref, *scratch, independent_rows):
    rows = x_ref.shape[0] // CHUNK
    x = jnp.concatenate([x_ref[pl.ds(t, rows, stride=CHUNK), :] for t in range(CHUNK)], axis=1)
    xb = x.astype(BF16)
    p = jnp.dot(xb, w_in_ref[0], preferred_element_type=F32)
    a_re = a_ref[0][:, :BLOCK_STATE]
    a_im = a_ref[0][:, BLOCK_STATE:]

    def step(sr, si, pr, pi):
        return a_re * sr - a_im * si + pr, a_re * si + a_im * sr + pi

    if independent_rows:
        s_prev = s0_ref[0]
        nr, ni = step(s_prev[:, :BLOCK_STATE], s_prev[:, BLOCK_STATE:],
                      p[:, :BLOCK_STATE], p[:, BLOCK_STATE:])
        sfin_ref[0] = jnp.concatenate([nr, ni], axis=1)
    else:
        state_sc, p_sc, prev_sc = scratch

        @pl.when(pl.program_id(1) == 0)
        def _():
            state_sc[...] = s0_ref[0]

        p_sc[...] = p

        def body(r, carry):
            sr, si = carry
            prev_sc[pl.ds(r, 1), :BLOCK_STATE] = sr
            prev_sc[pl.ds(r, 1), BLOCK_STATE:] = si
            return step(sr, si, p_sc[pl.ds(r, 1), :BLOCK_STATE], p_sc[pl.ds(r, 1), BLOCK_STATE:])

        sr, si = lax.fori_loop(0, rows, body,
                               (state_sc[:, :BLOCK_STATE], state_sc[:, BLOCK_STATE:]))
        state_sc[:, :BLOCK_STATE] = sr
        state_sc[:, BLOCK_STATE:] = si
        sfin_ref[0] = state_sc[...]
        s_prev = prev_sc[...]

    y = lax.dot_general(s_prev.astype(BF16), w_st_ref[0], (((1,), (1,)), ((), ())),
                        preferred_element_type=F32)
    y = y + jnp.dot(xb, w_loc_ref[0], preferred_element_type=F32)
    y = y + d_ref[0] * x
    for t in range(CHUNK):
        y_ref[pl.ds(t, rows, stride=CHUNK), :] = y[:, t * LANES:(t + 1) * LANES]


def _s5_branch(u_blk, w_in, w_st, w_loc, a_chunk, d_blk, s0, independent_rows, rows):
    n_chunks = u_blk.shape[1] // CHUNK
    n_seq = s0.shape[1]
    io_spec = pl.BlockSpec((None, rows * CHUNK, LANES), lambda c, t: (c, t, 0))
    wspec = pl.BlockSpec((1, BLOCK_IO, BLOCK_IO), lambda c, t: (c, 0, 0))
    vspec = lambda w: pl.BlockSpec((1, 1, w), lambda c, t: (c, 0, 0))
    scratch = [] if independent_rows else [
        pltpu.VMEM((1, 2 * BLOCK_STATE), F32),
        pltpu.VMEM((rows, 2 * BLOCK_STATE), F32),
        pltpu.VMEM((rows, 2 * BLOCK_STATE), F32),
    ]
    return pl.pallas_call(
        functools.partial(_s5_kernel, independent_rows=independent_rows),
        grid=(N_CH_BLOCKS, n_chunks // rows),
        in_specs=[
            io_spec,
            wspec, wspec, wspec,
            vspec(2 * BLOCK_STATE), vspec(BLOCK_IO),
            pl.BlockSpec((1, n_seq, 2 * BLOCK_STATE), lambda c, t: (c, 0, 0)),
        ],
        out_specs=[
            io_spec,
            pl.BlockSpec((1, n_seq, 2 * BLOCK_STATE), lambda c, t: (c, 0, 0)),
        ],
        out_shape=[
            jax.ShapeDtypeStruct(u_blk.shape, F32),
            jax.ShapeDtypeStruct(s0.shape, F32),
        ],
        scratch_shapes=scratch,
        compiler_params=_cparams(("arbitrary", "arbitrary")),
        name="s5_branch",
    )(u_blk, w_in, w_st, w_loc, a_chunk, d_blk, s0)


def _s5_operators(lam_re, lam_im, log_dt, b_re, b_im, c_re, c_im, d):
    g, gp, nb = N_GROUPS, GROUPS_PER_BLOCK, N_CH_BLOCKS
    dt = jnp.exp(log_dt)[:, None]
    kk = jnp.arange(CHUNK + 1, dtype=F32)[:, None, None]
    mag = jnp.exp(kk * (lam_re * dt)[None])
    ang = kk * (lam_im * dt)[None]
    ak_re, ak_im = mag * jnp.cos(ang), mag * jnp.sin(ang)
    ab_re, ab_im = ak_re[1], ak_im[1]
    den = lam_re * lam_re + lam_im * lam_im
    nr = ab_re - 1.0
    z_re = (nr * lam_re + ab_im * lam_im) / den
    z_im = (ab_im * lam_re - nr * lam_im) / den
    bb_re = z_re[..., None] * b_re - z_im[..., None] * b_im
    bb_im = z_re[..., None] * b_im + z_im[..., None] * b_re

    def group_rows(a):
        a = a.reshape(CHUNK, 2, nb, LANES, STATE_DIM).transpose(2, 0, 3, 1, 4)
        a = jnp.broadcast_to(a.reshape(nb, BLOCK_IO, 2, 1, STATE_DIM),
                             (nb, BLOCK_IO, 2, gp, STATE_DIM)).reshape(nb, BLOCK_IO, 2 * BLOCK_STATE)
        row_g = (np.arange(BLOCK_IO) % LANES) // GROUP_CH
        col_g = (np.arange(2 * BLOCK_STATE) % BLOCK_STATE) // STATE_DIM
        return jnp.where(jnp.asarray(row_g[:, None] == col_g[None, :]), a, 0.0).astype(BF16)

    kr = (CHUNK - 1.0) - kk[:CHUNK]
    mag_r = jnp.exp(kr * (lam_re * dt)[None])
    ang_r = kr * (lam_im * dt)[None]
    pw_re = (mag_r * jnp.cos(ang_r))[..., None]
    pw_im = (mag_r * jnp.sin(ang_r))[..., None]
    m_re = pw_re * bb_re[None] - pw_im * bb_im[None]
    m_im = pw_re * bb_im[None] + pw_im * bb_re[None]
    w_in = group_rows(jnp.stack([m_re, m_im], axis=1).transpose(0, 1, 2, 4, 3))

    e_re = c_re[None] * ak_re[1:, :, None, :] - c_im[None] * ak_im[1:, :, None, :]
    e_im = c_re[None] * ak_im[1:, :, None, :] + c_im[None] * ak_re[1:, :, None, :]
    w_st = group_rows(jnp.stack([e_re, -e_im], axis=1))

    ca_re = c_re[None] * ak_re[:CHUNK, :, None, :] - c_im[None] * ak_im[:CHUNK, :, None, :]
    ca_im = c_re[None] * ak_im[:CHUNK, :, None, :] + c_im[None] * ak_re[:CHUNK, :, None, :]
    bt_re = bb_re.transpose(0, 2, 1)[None, :, :, None]
    bt_im = bb_im.transpose(0, 2, 1)[None, :, :, None]
    kern = jnp.sum(ca_re[:, :, None] * bt_re - ca_im[:, :, None] * bt_im, axis=-1)
    kern = jnp.broadcast_to(kern.reshape(CHUNK, nb, LANES, 1, GROUP_CH),
                            (CHUNK, nb, LANES, gp, GROUP_CH)).reshape(CHUNK, nb, LANES, LANES)
    lane_g = np.arange(LANES) // GROUP_CH
    kern = jnp.where(jnp.asarray(lane_g[:, None] == lane_g[None, :]), kern, 0.0).astype(BF16)
    kern = jnp.concatenate([kern, jnp.zeros((1, nb, LANES, LANES), BF16)], axis=0)
    lag = np.arange(CHUNK)[None, :] - np.arange(CHUNK)[:, None]
    w_loc = kern[np.where(lag >= 0, lag, CHUNK)]
    w_loc = w_loc.transpose(2, 0, 3, 1, 4).reshape(nb, BLOCK_IO, BLOCK_IO)

    a_chunk = jnp.stack([ak_re[CHUNK], ak_im[CHUNK]], axis=0)
    a_chunk = a_chunk.reshape(2, nb, BLOCK_STATE).transpose(1, 0, 2).reshape(nb, 1, 2 * BLOCK_STATE)
    d_blk = jnp.tile(d.reshape(nb, 1, LANES), (1, 1, CHUNK))
    return w_in, w_st, w_loc, a_chunk, d_blk


def _state_to_blocks(s_re, s_im):
    n = s_re.shape[0]
    s = jnp.stack([s_re, s_im], axis=1).reshape(n, 2, N_CH_BLOCKS, BLOCK_STATE)
    return s.transpose(2, 0, 1, 3).reshape(N_CH_BLOCKS, n, 2 * BLOCK_STATE)


def _blocks_to_state(s):
    n = s.shape[1]
    s = s.reshape(N_CH_BLOCKS, n, 2, BLOCK_STATE).transpose(2, 1, 0, 3)
    s = s.reshape(2, n, N_GROUPS, STATE_DIM)
    return s[0], s[1]


def _lambda_value(lam_ref, lam_init):
    lv = lam_ref[...]
    t1 = jnp.sum(lv[0:1] * lv[1:2], axis=1, keepdims=True)
    t2 = jnp.sum(lv[2:3] * lv[3:4], axis=1, keepdims=True)
    return jnp.exp(t1) - jnp.exp(t2) + lam_init


def _split_subheads(q):
    lane = lax.broadcasted_iota(jnp.int32, q.shape, 1)
    zero = jnp.zeros_like(q)
    return jnp.concatenate([jnp.where(lane < HEAD_DIM, q, zero),
                            jnp.where(lane >= HEAD_DIM, q, zero)], axis=0)


def _finish_heads(o1, l1, o2, l2, lam, g, lam_init):
    of = o1 / l1 - lam * (o2 / l2)
    of = of * lax.rsqrt(jnp.mean(of * of, axis=-1, keepdims=True) + LN_EPS)
    return of * g * (1.0 - lam_init)


def _attn_prompt_kernel(lam_ref, g_ref, q_ref, k_ref, v_ref, o_ref, m_sc, acc_sc, *, tile, unroll,
                        lam_init):
    i = pl.program_id(1)
    qs = _split_subheads(q_ref[...])
    m_sc[...] = jnp.full(m_sc.shape, NEG_BIG, F32)
    acc_sc[...] = jnp.zeros(acc_sc.shape, F32)
    ones = jnp.ones((tile, LANES), BF16)

    def block(j, masked, m_old, acc):
        kj = k_ref[pl.ds(j * tile, tile), :]
        vj = jnp.concatenate([v_ref[pl.ds(j * tile, tile), :], ones], axis=1)
        s = lax.dot_general(qs, kj, (((1,), (1,)), ((), ())), preferred_element_type=F32)
        if masked:
            row = lax.broadcasted_iota(jnp.int32, s.shape, 0)
            col = lax.broadcasted_iota(jnp.int32, s.shape, 1)
            qpos = jnp.where(row >= tile, row - tile, row)
            s = jnp.where(col <= qpos, s, NEG_BIG)
        m_new = jnp.maximum(m_old, jnp.max(s, axis=1, keepdims=True))
        p = jnp.exp2(s - jnp.concatenate([m_new] * (tile // LANES), axis=1))
        scale = jnp.concatenate([jnp.exp2(m_old - m_new)] * (2 * V_DIM // LANES), axis=1)
        acc = scale * acc + jnp.dot(p.astype(BF16), vj, preferred_element_type=F32)
        return m_new, acc

    def run(j0, n):
        m, acc = m_sc[...], acc_sc[...]
        for t in range(n):
            m, acc = block(j0 + t, False, m, acc)
        m_sc[...] = m
        acc_sc[...] = acc

    def many(jj, c):
        run(unroll * jj, unroll)
        return c

    def single(j, c):
        run(j, 1)
        return c

    lax.fori_loop(0, i // unroll, many, 0)
    lax.fori_loop((i // unroll) * unroll, i, single, 0)
    _, acc = block(i, True, m_sc[...], acc_sc[...])
    lam = _lambda_value(lam_ref, lam_init)
    o_ref[...] = _finish_heads(acc[:tile, :V_DIM], acc[:tile, V_DIM:V_DIM + 1],
                               acc[tile:, :V_DIM], acc[tile:, V_DIM:V_DIM + 1],
                               lam, g_ref[...], lam_init)


def _attention_prompt(q_bf, k_bf, v_bf, lam_vec, g, lam_init, tile, unroll):
    seq = q_bf.shape[0]
    head_all = pl.BlockSpec((seq, 2 * HEAD_DIM), lambda h, i: (0, h))
    return pl.pallas_call(
        functools.partial(_attn_prompt_kernel, tile=tile, unroll=unroll, lam_init=lam_init),
        grid=(N_HEADS, seq // tile),
        in_specs=[
            pl.BlockSpec((4, HEAD_DIM), lambda h, i: (0, 0)),
            pl.BlockSpec((1, V_DIM), lambda h, i: (0, 0)),
            pl.BlockSpec((tile, 2 * HEAD_DIM), lambda h, i: (i, h)),
            head_all, head_all,
        ],
        out_specs=pl.BlockSpec((tile, V_DIM), lambda h, i: (i, h)),
        out_shape=jax.ShapeDtypeStruct((seq, D_ATT), F32),
        scratch_shapes=[pltpu.VMEM((2 * tile, LANES), F32), pltpu.VMEM((2 * tile, 2 * V_DIM), F32)],
        compiler_params=_cparams(("arbitrary", "arbitrary")),
        name="attention_prompt",
    )(lam_vec, g, q_bf, k_bf, v_bf)


def _attn_sample_kernel(pt_ref, lam_ref, g_ref, bias_ref, q_ref, kn_ref, vn_ref, *rest,
                        pages, group, n_new, lam_init):
    del pt_ref
    k_refs = rest[:pages]
    v_refs = rest[pages:2 * pages]
    o_ref, m_sc, l_sc, acc_sc = rest[2 * pages:]
    c = pl.program_id(1)
    sub_rows = 2 * n_new

    @pl.when(c == 0)
    def _():
        m_sc[...] = jnp.full(m_sc.shape, NEG_BIG, F32)
        l_sc[...] = jnp.zeros(l_sc.shape, F32)
        acc_sc[...] = jnp.zeros(acc_sc.shape, F32)

    def update(s_list, v_list):
        m_old = m_sc[...]
        m_new = m_old
        for s in s_list:
            m_new = jnp.maximum(m_new, jnp.max(s, axis=1, keepdims=True))
        scale = jnp.exp2(m_old - m_new)
        l = scale * l_sc[...]
        acc = scale * acc_sc[...]
        for s, vals in zip(s_list, v_list):
            p = jnp.exp2(s - m_new)
            l = l + jnp.sum(p, axis=1, keepdims=True)
            acc = acc + jnp.dot(p.astype(BF16), vals, preferred_element_type=F32)
        l_sc[...] = l
        acc_sc[...] = acc
        m_sc[...] = m_new

    def scores(keys):
        return lax.dot_general(qs, keys, (((1,), (1,)), ((), ())), preferred_element_type=F32)

    q = q_ref[0].astype(F32)
    qs = jnp.concatenate([_split_subheads(q[:, h * V_DIM:(h + 1) * V_DIM]) for h in range(N_HEADS)],
                         axis=0).astype(BF16)
    bias = bias_ref[...]
    s_list, v_list = [], []
    for g0 in range(0, pages, group):
        keys = jnp.concatenate([r[...] for r in k_refs[g0:g0 + group]], axis=0).astype(BF16)
        v_list.append(jnp.concatenate([r[...] for r in v_refs[g0:g0 + group]], axis=0).astype(BF16))
        s_list.append(scores(keys) + bias)
    update(s_list, v_list)

    @pl.when(c == pl.num_programs(1) - 1)
    def _():
        lam = _lambda_value(lam_ref, lam_init)
        n_rows = n_new * N_HEADS
        pad = jnp.zeros((LANES - n_rows, V_DIM), F32)
        keys = jnp.concatenate([kn_ref[...], pad], axis=0).astype(BF16)
        vals = jnp.concatenate([vn_ref[...], pad], axis=0).astype(BF16)
        s = scores(keys)
        row = lax.broadcasted_iota(jnp.int32, s.shape, 0)
        col = lax.broadcasted_iota(jnp.int32, s.shape, 1)
        ok = ((col % N_HEADS == row // sub_rows) & (col // N_HEADS <= row % n_new) & (col < n_rows))
        update([jnp.where(ok, s, NEG_BIG)], [vals])
        acc = acc_sc[...]
        l = l_sc[...]
        for h in range(N_HEADS):
            r0 = h * sub_rows
            o_ref[0, :, h * V_DIM:(h + 1) * V_DIM] = _finish_heads(
                acc[r0:r0 + n_new], l[r0:r0 + n_new],
                acc[r0 + n_new:r0 + sub_rows], l[r0 + n_new:r0 + sub_rows], lam, g_ref[...], lam_init)


def _attention_sample(page_table, q_bf, k_new, v_new, cache_k, cache_v, layer, lam_vec, g,
                      lam_init, pages, group):
    batch, n_new, _ = q_bf.shape
    n_pages = page_table.shape[1]
    page_rows = PAGE_SIZE * N_HEADS
    q_rows = N_HEADS * 2 * n_new
    per_b = lambda r, w: pl.BlockSpec((1, r, w), lambda b, c, pt: (b, 0, 0))
    new_rows = pl.BlockSpec((None, n_new * N_HEADS, V_DIM), lambda b, c, pt: (layer, b, 0))
    same_head = (np.arange(group * page_rows)[None, :] % N_HEADS) == (np.arange(q_rows)[:, None] // (2 * n_new))
    bias = jnp.asarray(np.where(same_head, 0.0, NEG_BIG), F32)

    def page_spec(i):
        return pl.BlockSpec((None, None, page_rows, V_DIM),
                            lambda b, c, pt: (layer, pt[b, c * pages + i], 0, 0))

    grid_spec = pltpu.PrefetchScalarGridSpec(
        num_scalar_prefetch=1,
        grid=(batch, n_pages // pages),
        in_specs=[
            pl.BlockSpec((4, HEAD_DIM), lambda b, c, pt: (0, 0)),
            pl.BlockSpec((1, V_DIM), lambda b, c, pt: (0, 0)),
            pl.BlockSpec(bias.shape, lambda b, c, pt: (0, 0)),
            per_b(n_new, D_QK), new_rows, new_rows,
        ] + [page_spec(i) for i in range(pages)] * 2,
        out_specs=per_b(n_new, D_ATT),
        scratch_shapes=[
            pltpu.VMEM((q_rows, 1), F32),
            pltpu.VMEM((q_rows, 1), F32),
            pltpu.VMEM((q_rows, V_DIM), F32),
        ],
    )
    return pl.pallas_call(
        functools.partial(_attn_sample_kernel, pages=pages, group=group, n_new=n_new,
                          lam_init=lam_init),
        grid_spec=grid_spec,
        out_shape=jax.ShapeDtypeStruct((batch, n_new, D_ATT), F32),
        compiler_params=_cparams(("arbitrary", "arbitrary")),
        name="attention_sample",
    )(page_table, lam_vec, g, bias, q_bf, k_new, v_new, *([cache_k] * pages), *([cache_v] * pages))


def _sigmoid(x):
    return 1.0 / (1.0 + jnp.exp(-x))


def _gelu_tanh(x):
    return 0.5 * x * (1.0 + jnp.tanh(math.sqrt(2.0 / math.pi) * (x + 0.044715 * (x * x * x))))


def _tail_kernel(x_ref, y_ref, gates_ref, o_ref, wg_ref, bg_ref, wps_ref, wpa_ref, wo_ref,
                 lng_ref, lnb_ref, out_ref):
    ys = jnp.concatenate([y_ref[cb] for cb in range(N_CH_BLOCKS)], axis=1)
    ys = _gelu_tanh(ys)
    glu = jnp.dot(ys.astype(BF16), wg_ref[...], preferred_element_type=F32) + bg_ref[...]
    ys = ys * _sigmoid(glu)
    gate_s = gates_ref[:, 0:D_SSM]
    ys = ys * (gate_s * _sigmoid(gate_s))
    gate_a = gates_ref[:, D_SSM:2 * D_SSM]
    ya = o_ref[...] * (gate_a * _sigmoid(gate_a))
    zg_s = gates_ref[:, 2 * D_SSM:2 * D_SSM + D_MODEL]
    zg_a = gates_ref[:, 2 * D_SSM + D_MODEL:]
    merged = (_sigmoid(zg_s) * jnp.dot(ys.astype(BF16), wps_ref[...], preferred_element_type=F32)
              + _sigmoid(zg_a) * jnp.dot(ya.astype(BF16), wpa_ref[...], preferred_element_type=F32))
    out = jnp.dot(merged.astype(BF16), wo_ref[...], preferred_element_type=F32)
    r = DEEPNORM_ALPHA * x_ref[...] + out
    mu = jnp.mean(r, axis=-1, keepdims=True)
    rc = r - mu
    var = jnp.mean(rc * rc, axis=-1, keepdims=True)
    out_ref[...] = rc * lax.rsqrt(var + LN_EPS) * lng_ref[...] + lnb_ref[...]


def _tail(x, y_blk, gates, o, w_glu, b_glu, w_ps, w_pa, w_o, ln_g, ln_b, tm):
    m = x.shape[0]
    row = lambda i: (i, 0)
    const = lambda shape: pl.BlockSpec(shape, lambda i: (0, 0), pipeline_mode=pl.Buffered(1))
    return pl.pallas_call(
        _tail_kernel,
        grid=(m // tm,),
        in_specs=[
            pl.BlockSpec((tm, D_MODEL), row),
            pl.BlockSpec((N_CH_BLOCKS, tm, LANES), lambda i: (0, i, 0)),
            pl.BlockSpec((tm, N_GATES), row),
            pl.BlockSpec((tm, D_ATT), row),
            const((D_SSM, D_SSM)), const((1, D_SSM)),
            const((D_SSM, D_MODEL)), const((D_ATT, D_MODEL)), const((D_MODEL, D_MODEL)),
            const((1, D_MODEL)), const((1, D_MODEL)),
        ],
        out_specs=pl.BlockSpec((tm, D_MODEL), row),
        out_shape=jax.ShapeDtypeStruct((m, D_MODEL), F32),
        compiler_params=_cparams(("arbitrary",)),
        name="merge_tail",
    )(x, y_blk, gates, o, w_glu, b_glu, w_ps, w_pa, w_o, ln_g, ln_b)


def _rope_tables(pos):
    half = ROT_DIM // 2
    inv = ROPE_THETA ** (-jnp.arange(0, ROT_DIM, 2, dtype=F32) / ROT_DIM)
    ang = pos.astype(F32)[:, None] * inv[None, :]
    cos, sin = jnp.cos(ang), jnp.sin(ang)
    n = pos.shape[0]
    pad = HEAD_DIM - ROT_DIM
    c = jnp.concatenate([cos, cos, jnp.ones((n, pad), F32)], axis=1)
    sa = jnp.concatenate([-sin, jnp.zeros((n, half + pad), F32)], axis=1)
    sb = jnp.concatenate([jnp.zeros((n, half), F32), sin, jnp.zeros((n, pad), F32)], axis=1)
    rep = LANES // HEAD_DIM
    return jnp.tile(c, (1, rep)), jnp.tile(sa, (1, rep)), jnp.tile(sb, (1, rep))


def kernel(x_prompt, x_sample, cache_k, cache_v, state_ssm_re, state_ssm_im, page_table, w_in, ssm_lambda_re, ssm_lambda_im, ssm_log_dt, ssm_b_re, ssm_b_im, ssm_c_re, ssm_c_im, ssm_d, w_glu, b_glu, lam_q1, lam_k1, lam_q2, lam_k2, subln_g, w_proj_s, w_proj_a, w_out, ln_g, ln_b):
    bsz_p, seq_p, _ = x_prompt.shape
    bsz_s, seq_s, _ = x_sample.shape
    assert bsz_p == 1 and seq_s == CHUNK
    n_pages = page_table.shape[1]
    past_len = n_pages * PAGE_SIZE
    m_p, m_s = bsz_p * seq_p, bsz_s * seq_s

    rope_p = _rope_tables(jnp.arange(seq_p))
    rope_s = tuple(jnp.tile(t, (bsz_s, 1)) for t in _rope_tables(past_len + jnp.arange(seq_s)))

    ck = cache_k.reshape(DEPTH, cache_k.shape[1], PAGE_SIZE * N_HEADS, V_DIM)
    cv = cache_v.reshape(DEPTH, cache_v.shape[1], PAGE_SIZE * N_HEADS, V_DIM)

    xp = x_prompt.reshape(m_p, D_MODEL)
    xs = x_sample.reshape(m_s, D_MODEL)
    zero_state = jnp.zeros((N_CH_BLOCKS, bsz_p, 2 * BLOCK_STATE), F32)
    kp_all = jnp.zeros((DEPTH, m_p * N_HEADS, V_DIM), F32)
    vp_all = jnp.zeros((DEPTH, m_p * N_HEADS, V_DIM), F32)
    ks_all = jnp.zeros((DEPTH, m_s * N_HEADS, V_DIM), F32)
    vs_all = jnp.zeros((DEPTH, m_s * N_HEADS, V_DIM), F32)
    outs = [[] for _ in range(4)]
    for l in range(DEPTH):
        lam_init = 0.8 - 0.6 * math.exp(-0.3 * l)
        w_bf = w_in[l].astype(BF16)
        ops = _s5_operators(ssm_lambda_re[l], ssm_lambda_im[l], ssm_log_dt[l], ssm_b_re[l],
                            ssm_b_im[l], ssm_c_re[l], ssm_c_im[l], ssm_d[l])
        lam_vec = jnp.stack([lam_q1[l], lam_k1[l], lam_q2[l], lam_k2[l]], axis=0)
        g = subln_g[l].reshape(1, V_DIM)
        tail_w = (w_glu[l].astype(BF16), b_glu[l].reshape(1, D_SSM), w_proj_s[l].astype(BF16),
                  w_proj_a[l].astype(BF16), w_out[l].astype(BF16),
                  ln_g[l].reshape(1, D_MODEL), ln_b[l].reshape(1, D_MODEL))

        u, gates, q_bf, kp_all, k_bf, vp_all, v_bf = _in_projection(
            xp, w_bf, *rope_p, kp_all, vp_all, l, tm=512)
        y, sfin = _s5_branch(u, *ops, zero_state, independent_rows=False, rows=256)
        o = _attention_prompt(q_bf, k_bf, v_bf, lam_vec, g, lam_init, tile=512, unroll=4)
        xp = _tail(xp, y, gates, o, *tail_w, tm=512)
        sr, si = _blocks_to_state(sfin)
        outs[0].append(sr)
        outs[1].append(si)

        u, gates, q_bf, ks_all, _, vs_all, _ = _in_projection(
            xs, w_bf, *rope_s, ks_all, vs_all, l, tm=m_s)
        s0 = _state_to_blocks(state_ssm_re[l], state_ssm_im[l])
        y, sfin = _s5_branch(u, *ops, s0, independent_rows=True, rows=bsz_s)
        o = _attention_sample(page_table, q_bf.reshape(bsz_s, seq_s, D_QK), ks_all, vs_all,
                              ck, cv, l, lam_vec, g, lam_init, pages=16, group=4)
        xs = _tail(xs, y, gates, o.reshape(m_s, D_ATT), *tail_w, tm=m_s)
        sr, si = _blocks_to_state(sfin)
        outs[2].append(sr)
        outs[3].append(si)

    kv_p = (DEPTH, bsz_p, seq_p, N_HEADS, V_DIM)
    kv_s = (DEPTH, bsz_s, seq_s, N_HEADS, V_DIM)
    return (xp.reshape(bsz_p, seq_p, D_MODEL), xs.reshape(bsz_s, seq_s, D_MODEL),
            kp_all.reshape(kv_p), vp_all.reshape(kv_p), jnp.stack(outs[0]), jnp.stack(outs[1]),
            ks_all.reshape(kv_s), vs_all.reshape(kv_s), jnp.stack(outs[2]), jnp.stack(outs[3]))
```

```python
import functools
import math

import jax
import jax.numpy as jnp
import numpy as np
from jax import lax
from jax.experimental import pallas as pl
from jax.experimental.pallas import tpu as pltpu

F32 = jnp.float32
BF16 = jnp.bfloat16

D_MODEL = 1024
DEPTH = 2
PAGE_SIZE = 128
D_SSM = D_MODEL // 2
GROUP_CH = 16
N_GROUPS = D_SSM // GROUP_CH
STATE_DIM = 64
N_HEADS = 4
HEAD_DIM = D_MODEL // (4 * N_HEADS)
V_DIM = 2 * HEAD_DIM
D_QK = N_HEADS * 2 * HEAD_DIM
D_ATT = N_HEADS * V_DIM
ROT_DIM = HEAD_DIM // 4
ROPE_THETA = 500000.0
DEEPNORM_ALPHA = (2 * DEPTH) ** 0.25
LN_EPS = 1e-5
D_IN = 2 * D_SSM + 2 * D_QK + 2 * D_ATT + 2 * D_MODEL

LANES = 128
SUBLANES = 8

CHUNK = SUBLANES
GROUPS_PER_BLOCK = LANES // GROUP_CH
N_CH_BLOCKS = D_SSM // LANES
BLOCK_STATE = GROUPS_PER_BLOCK * STATE_DIM
BLOCK_IO = CHUNK * LANES

NEG_BIG = -1e30
LOG2_E = 1.4426950408889634
VMEM_LIMIT = 56 * 1024 * 1024

_OFF_U = 0
_OFF_GS = _OFF_U + D_SSM
_OFF_Q = _OFF_GS + D_SSM
_OFF_K = _OFF_Q + D_QK
_OFF_V = _OFF_K + D_QK
_OFF_GA = _OFF_V + D_ATT
_OFF_ZS = _OFF_GA + D_ATT
_OFF_ZA = _OFF_ZS + D_MODEL
N_GATES = 2 * D_SSM + 2 * D_MODEL


def _cparams(semantics):
    return pltpu.CompilerParams(dimension_semantics=semantics, vmem_limit_bytes=VMEM_LIMIT)


def _inproj_kernel(x_ref, w_ref, cos_ref, sa_ref, sb_ref, kf_in, vf_in,
                   u_ref, gates_ref, q_ref, kf_ref, kb_ref, vf_ref, vb_ref):
    del kf_in, vf_in
    tm = x_ref.shape[0]
    xb = x_ref[...].astype(BF16)

    def seg(off, width):
        return jnp.dot(xb, w_ref[:, off:off + width], preferred_element_type=F32)

    u = seg(_OFF_U, D_SSM)
    for cb in range(N_CH_BLOCKS):
        u_ref[cb] = u[:, cb * LANES:(cb + 1) * LANES]
    gates_ref[:, 0:D_SSM] = seg(_OFF_GS, D_SSM)
    gates_ref[:, D_SSM:2 * D_SSM] = seg(_OFF_GA, D_ATT)
    gates_ref[:, 2 * D_SSM:2 * D_SSM + D_MODEL] = seg(_OFF_ZS, D_MODEL)
    gates_ref[:, 2 * D_SSM + D_MODEL:] = seg(_OFF_ZA, D_MODEL)

    reps = D_QK // LANES
    cos = jnp.concatenate([cos_ref[...]] * reps, axis=1)
    sa = jnp.concatenate([sa_ref[...]] * reps, axis=1)
    sb = jnp.concatenate([sb_ref[...]] * reps, axis=1)
    half = ROT_DIM // 2

    def rope(t):
        return (t * cos + pltpu.roll(t, D_QK - half, 1) * sa + pltpu.roll(t, half, 1) * sb)

    q = rope(seg(_OFF_Q, D_QK))
    q_ref[...] = (q * (HEAD_DIM ** -0.5 * LOG2_E)).astype(BF16)
    k = rope(seg(_OFF_K, D_QK))
    kb_ref[...] = k.astype(BF16)
    v = seg(_OFF_V, D_ATT)
    vb_ref[...] = v.astype(BF16)
    for h in range(N_HEADS):
        kf_ref[pl.ds(h, tm, stride=N_HEADS), :] = k[:, h * V_DIM:(h + 1) * V_DIM]
        vf_ref[pl.ds(h, tm, stride=N_HEADS), :] = v[:, h * V_DIM:(h + 1) * V_DIM]


def _in_projection(x, w_bf, cos_t, sa_t, sb_t, kf_all, vf_all, layer, tm):
    m = x.shape[0]
    row = lambda i: (i, 0)
    kv_rows = pl.BlockSpec((None, tm * N_HEADS, V_DIM), lambda i: (layer, i, 0))
    return pl.pallas_call(
        _inproj_kernel,
        grid=(m // tm,),
        in_specs=[
            pl.BlockSpec((tm, D_MODEL), row),
            pl.BlockSpec((D_MODEL, D_IN), lambda i: (0, 0), pipeline_mode=pl.Buffered(1)),
            pl.BlockSpec((tm, LANES), row),
            pl.BlockSpec((tm, LANES), row),
            pl.BlockSpec((tm, LANES), row),
            pl.BlockSpec(memory_space=pl.ANY),
            pl.BlockSpec(memory_space=pl.ANY),
        ],
        out_specs=[
            pl.BlockSpec((N_CH_BLOCKS, tm, LANES), lambda i: (0, i, 0)),
            pl.BlockSpec((tm, N_GATES), row),
            pl.BlockSpec((tm, D_QK), row),
            kv_rows,
            pl.BlockSpec((tm, D_QK), row),
            kv_rows,
            pl.BlockSpec((tm, D_ATT), row),
        ],
        out_shape=[
            jax.ShapeDtypeStruct((N_CH_BLOCKS, m, LANES), F32),
            jax.ShapeDtypeStruct((m, N_GATES), F32),
            jax.ShapeDtypeStruct((m, D_QK), BF16),
            jax.ShapeDtypeStruct(kf_all.shape, F32),
            jax.ShapeDtypeStruct((m, D_QK), BF16),
            jax.ShapeDtypeStruct(vf_all.shape, F32),
            jax.ShapeDtypeStruct((m, D_ATT), BF16),
        ],
        input_output_aliases={5: 3, 6: 5},
        compiler_params=_cparams(("arbitrary",)),
        name="in_projection",
    )(x, w_bf, cos_t, sa_t, sb_t, kf_all, vf_all)


def _s5_kernel(x_ref, w_in_ref, w_st_ref, w_loc_ref, a_ref, d_ref, s0_ref,
               y_ref, sfin_ref, *scratch, independent_rows):
    rows = x_ref.shape[0] // CHUNK
    x = jnp.concatenate([x_ref[pl.ds(t, rows, stride=CHUNK), :] for t in range(CHUNK)], axis=1)
    xb = x.astype(BF16)
    p = jnp.dot(xb, w_in_ref[0], preferred_element_type=F32)
    a_re = a_ref[0][:, :BLOCK_STATE]
    a_im = a_ref[0][:, BLOCK_STATE:]

    def step(sr, si, pr, pi):
        return a_re * sr - a_im * si + pr, a_re * si + a_im * sr + pi

    if independent_rows:
        s_prev = s0_ref[0]
        nr, ni = step(s_prev[:, :BLOCK_STATE], s_prev[:, BLOCK_STATE:],
                      p[:, :BLOCK_STATE], p[:, BLOCK_STATE:])
        sfin_ref[0] = jnp.concatenate([nr, ni], axis=1)
    else:
        state_sc, p_sc, prev_sc = scratch

        @pl.when(pl.program_id(1) == 0)
        def _():
            state_sc[...] = s0_ref[0]

        p_sc[...] = p

        def body(r, carry):
            sr, si = carry
            prev_sc[pl.ds(r, 1), :BLOCK_STATE] = sr
            prev_sc[pl.ds(r, 1), BLOCK_STATE:] = si
            return step(sr, si, p_sc[pl.ds(r, 1), :BLOCK_STATE], p_sc[pl.ds(r, 1), BLOCK_STATE:])

        sr, si = lax.fori_loop(0, rows, body,
                               (state_sc[:, :BLOCK_STATE], state_sc[:, BLOCK_STATE:]))
        state_sc[:, :BLOCK_STATE] = sr
        state_sc[:, BLOCK_STATE:] = si
        sfin_ref[0] = state_sc[...]
        s_prev = prev_sc[...]

    y = lax.dot_general(s_prev.astype(BF16), w_st_ref[0], (((1,), (1,)), ((), ())),
                        preferred_element_type=F32)
    y = y + jnp.dot(xb, w_loc_ref[0], preferred_element_type=F32)
    y = y + d_ref[0] * x
    for t in range(CHUNK):
        y_ref[pl.ds(t, rows, stride=CHUNK), :] = y[:, t * LANES:(t + 1) * LANES]


def _s5_branch(u_blk, w_in, w_st, w_loc, a_chunk, d_blk, s0, independent_rows, rows):
    n_chunks = u_blk.shape[1] // CHUNK
    n_seq = s0.shape[1]
    io_spec = pl.BlockSpec((None, rows * CHUNK, LANES), lambda c, t: (c, t, 0))
    wspec = pl.BlockSpec((1, BLOCK_IO, BLOCK_IO), lambda c, t: (c, 0, 0))
    vspec = lambda w: pl.BlockSpec((1, 1, w), lambda c, t: (c, 0, 0))
    scratch = [] if independent_rows else [
        pltpu.VMEM((1, 2 * BLOCK_STATE), F32),
        pltpu.VMEM((rows, 2 * BLOCK_STATE), F32),
        pltpu.VMEM((rows, 2 * BLOCK_STATE), F32),
    ]
    return pl.pallas_call(
        functools.partial(_s5_kernel, independent_rows=independent_rows),
        grid=(N_CH_BLOCKS, n_chunks // rows),
        in_specs=[
            io_spec,
            wspec, wspec, wspec,
            vspec(2 * BLOCK_STATE), vspec(BLOCK_IO),
            pl.BlockSpec((1, n_seq, 2 * BLOCK_STATE), lambda c, t: (c, 0, 0)),
        ],
        out_specs=[
            io_spec,
            pl.BlockSpec((1, n_seq, 2 * BLOCK_STATE), lambda c, t: (c, 0, 0)),
        ],
        out_shape=[
            jax.ShapeDtypeStruct(u_blk.shape, F32),
            jax.ShapeDtypeStruct(s0.shape, F32),
        ],
        scratch_shapes=scratch,
        compiler_params=_cparams(("arbitrary", "arbitrary")),
        name="s5_branch",
    )(u_blk, w_in, w_st, w_loc, a_chunk, d_blk, s0)


def _s5_operators(lam_re, lam_im, log_dt, b_re, b_im, c_re, c_im, d):
    g, gp, nb = N_GROUPS, GROUPS_PER_BLOCK, N_CH_BLOCKS
    dt = jnp.exp(log_dt)[:, None]
    kk = jnp.arange(CHUNK + 1, dtype=F32)[:, None, None]
    mag = jnp.exp(kk * (lam_re * dt)[None])
    ang = kk * (lam_im * dt)[None]
    ak_re, ak_im = mag * jnp.cos(ang), mag * jnp.sin(ang)
    ab_re, ab_im = ak_re[1], ak_im[1]
    den = lam_re * lam_re + lam_im * lam_im
    nr = ab_re - 1.0
    z_re = (nr * lam_re + ab_im * lam_im) / den
    z_im = (ab_im * lam_re - nr * lam_im) / den
    bb_re = z_re[..., None] * b_re - z_im[..., None] * b_im
    bb_im = z_re[..., None] * b_im + z_im[..., None] * b_re

    def group_rows(a):
        a = a.reshape(CHUNK, 2, nb, LANES, STATE_DIM).transpose(2, 0, 3, 1, 4)
        a = jnp.broadcast_to(a.reshape(nb, BLOCK_IO, 2, 1, STATE_DIM),
                             (nb, BLOCK_IO, 2, gp, STATE_DIM)).reshape(nb, BLOCK_IO, 2 * BLOCK_STATE)
        row_g = (np.arange(BLOCK_IO) % LANES) // GROUP_CH
        col_g = (np.arange(2 * BLOCK_STATE) % BLOCK_STATE) // STATE_DIM
        return jnp.where(jnp.asarray(row_g[:, None] == col_g[None, :]), a, 0.0).astype(BF16)

    kr = (CHUNK - 1.0) - kk[:CHUNK]
    mag_r = jnp.exp(kr * (lam_re * dt)[None])
    ang_r = kr * (lam_im * dt)[None]
    pw_re = (mag_r * jnp.cos(ang_r))[..., None]
    pw_im = (mag_r * jnp.sin(ang_r))[..., None]
    m_re = pw_re * bb_re[None] - pw_im * bb_im[None]
    m_im = pw_re * bb_im[None] + pw_im * bb_re[None]
    w_in = group_rows(jnp.stack([m_re, m_im], axis=1).transpose(0, 1, 2, 4, 3))

    e_re = c_re[None] * ak_re[1:, :, None, :] - c_im[None] * ak_im[1:, :, None, :]
    e_im = c_re[None] * ak_im[1:, :, None, :] + c_im[None] * ak_re[1:, :, None, :]
    w_st = group_rows(jnp.stack([e_re, -e_im], axis=1))

    ca_re = c_re[None] * ak_re[:CHUNK, :, None, :] - c_im[None] * ak_im[:CHUNK, :, None, :]
    ca_im = c_re[None] * ak_im[:CHUNK, :, None, :] + c_im[None] * ak_re[:CHUNK, :, None, :]
    bt_re = bb_re.transpose(0, 2, 1)[None, :, :, None]
    bt_im = bb_im.transpose(0, 2, 1)[None, :, :, None]
    kern = jnp.sum(ca_re[:, :, None] * bt_re - ca_im[:, :, None] * bt_im, axis=-1)
    kern = jnp.broadcast_to(kern.reshape(CHUNK, nb, LANES, 1, GROUP_CH),
                            (CHUNK, nb, LANES, gp, GROUP_CH)).reshape(CHUNK, nb, LANES, LANES)
    lane_g = np.arange(LANES) // GROUP_CH
    kern = jnp.where(jnp.asarray(lane_g[:, None] == lane_g[None, :]), kern, 0.0).astype(BF16)
    kern = jnp.concatenate([kern, jnp.zeros((1, nb, LANES, LANES), BF16)], axis=0)
    lag = np.arange(CHUNK)[None, :] - np.arange(CHUNK)[:, None]
    w_loc = kern[np.where(lag >= 0, lag, CHUNK)]
    w_loc = w_loc.transpose(2, 0, 3, 1, 4).reshape(nb, BLOCK_IO, BLOCK_IO)

    a_chunk = jnp.stack([ak_re[CHUNK], ak_im[CHUNK]], axis=0)
    a_chunk = a_chunk.reshape(2, nb, BLOCK_STATE).transpose(1, 0, 2).reshape(nb, 1, 2 * BLOCK_STATE)
    d_blk = jnp.tile(d.reshape(nb, 1, LANES), (1, 1, CHUNK))
    return w_in, w_st, w_loc, a_chunk, d_blk


def _state_to_blocks(s_re, s_im):
    n = s_re.shape[0]
    s = jnp.stack([s_re, s_im], axis=1).reshape(n, 2, N_CH_BLOCKS, BLOCK_STATE)
    return s.transpose(2, 0, 1, 3).reshape(N_CH_BLOCKS, n, 2 * BLOCK_STATE)


def _blocks_to_state(s):
    n = s.shape[1]
    s = s.reshape(N_CH_BLOCKS, n, 2, BLOCK_STATE).transpose(2, 1, 0, 3)
    s = s.reshape(2, n, N_GROUPS, STATE_DIM)
    return s[0], s[1]


def _lambda_value(lam_ref, lam_init):
    lv = lam_ref[...]
    t1 = jnp.sum(lv[0:1] * lv[1:2], axis=1, keepdims=True)
    t2 = jnp.sum(lv[2:3] * lv[3:4], axis=1, keepdims=True)
    return jnp.exp(t1) - jnp.exp(t2) + lam_init


def _split_subheads(q):
    lane = lax.broadcasted_iota(jnp.int32, q.shape, 1)
    zero = jnp.zeros_like(q)
    return jnp.concatenate([jnp.where(lane < HEAD_DIM, q, zero),
                            jnp.where(lane >= HEAD_DIM, q, zero)], axis=0)


def _finish_heads(o1, l1, o2, l2, lam, g, lam_init):
    of = o1 / l1 - lam * (o2 / l2)
    of = of * lax.rsqrt(jnp.mean(of * of, axis=-1, keepdims=True) + LN_EPS)
    return of * g * (1.0 - lam_init)


def _attn_prompt_kernel(lam_ref, g_ref, q_ref, k_ref, v_ref, o_ref, m_sc, acc_sc, *, tile, unroll,
                        lam_init):
    i = pl.program_id(1)
    qs = _split_subheads(q_ref[...])
    m_sc[...] = jnp.full(m_sc.shape, NEG_BIG, F32)
    acc_sc[...] = jnp.zeros(acc_sc.shape, F32)
    ones = jnp.ones((tile, LANES), BF16)

    def block(j, masked, m_old, acc):
        kj = k_ref[pl.ds(j * tile, tile), :]
        vj = jnp.concatenate([v_ref[pl.ds(j * tile, tile), :], ones], axis=1)
        s = lax.dot_general(qs, kj, (((1,), (1,)), ((), ())), preferred_element_type=F32)
        if masked:
            row = lax.broadcasted_iota(jnp.int32, s.shape, 0)
            col = lax.broadcasted_iota(jnp.int32, s.shape, 1)
            qpos = jnp.where(row >= tile, row - tile, row)
            s = jnp.where(col <= qpos, s, NEG_BIG)
        m_new = jnp.maximum(m_old, jnp.max(s, axis=1, keepdims=True))
        p = jnp.exp2((s - jnp.concatenate([m_new] * (tile // LANES), axis=1)).astype(BF16))
        scale = jnp.concatenate([jnp.exp2(m_old - m_new)] * (2 * V_DIM // LANES), axis=1)
        acc = scale * acc + jnp.dot(p, vj, preferred_element_type=F32)
        return m_new, acc

    def run(j0, n):
        m, acc = m_sc[...], acc_sc[...]
        for t in range(n):
            m, acc = block(j0 + t, False, m, acc)
        m_sc[...] = m
        acc_sc[...] = acc

    def many(jj, c):
        run(unroll * jj, unroll)
        return c

    lax.fori_loop(0, i // unroll, many, 0)
    half = unroll // 2
    rem = i % unroll

    @pl.when(rem >= half)
    def _():
        run(i - rem, half)

    for r in range(half):
        @pl.when(rem % half == r)
        def _():
            m, acc = m_sc[...], acc_sc[...]
            for t in range(r):
                m, acc = block(i - r + t, False, m, acc)
            _, acc = block(i, True, m, acc)
            lam = _lambda_value(lam_ref, lam_init)
            o_ref[...] = _finish_heads(acc[:tile, :V_DIM], acc[:tile, V_DIM:V_DIM + 1],
                                       acc[tile:, :V_DIM], acc[tile:, V_DIM:V_DIM + 1],
                                       lam, g_ref[...], lam_init)


def _attention_prompt(q_bf, k_bf, v_bf, lam_vec, g, lam_init, tile, unroll):
    seq = q_bf.shape[0]
    head_all = pl.BlockSpec((seq, 2 * HEAD_DIM), lambda h, i: (0, h))
    return pl.pallas_call(
        functools.partial(_attn_prompt_kernel, tile=tile, unroll=unroll, lam_init=lam_init),
        grid=(N_HEADS, seq // tile),
        in_specs=[
            pl.BlockSpec((4, HEAD_DIM), lambda h, i: (0, 0)),
            pl.BlockSpec((1, V_DIM), lambda h, i: (0, 0)),
            pl.BlockSpec((tile, 2 * HEAD_DIM), lambda h, i: (i, h)),
            head_all, head_all,
        ],
        out_specs=pl.BlockSpec((tile, V_DIM), lambda h, i: (i, h)),
        out_shape=jax.ShapeDtypeStruct((seq, D_ATT), F32),
        scratch_shapes=[pltpu.VMEM((2 * tile, LANES), F32), pltpu.VMEM((2 * tile, 2 * V_DIM), F32)],
        compiler_params=_cparams(("arbitrary", "arbitrary")),
        name="attention_prompt",
    )(lam_vec, g, q_bf, k_bf, v_bf)


def _attn_sample_kernel(pt_ref, lam_ref, g_ref, bias_ref, q_ref, kn_ref, vn_ref, *rest,
                        pages, group, n_new, lam_init):
    del pt_ref
    k_refs = rest[:pages]
    v_refs = rest[pages:2 * pages]
    o_ref, m_sc, l_sc, acc_sc = rest[2 * pages:]
    c = pl.program_id(1)
    sub_rows = 2 * n_new

    @pl.when(c == 0)
    def _():
        m_sc[...] = jnp.full(m_sc.shape, NEG_BIG, F32)
        l_sc[...] = jnp.zeros(l_sc.shape, F32)
        acc_sc[...] = jnp.zeros(acc_sc.shape, F32)

    def update(s_list, v_list):
        m_old = m_sc[...]
        m_new = m_old
        for s in s_list:
            m_new = jnp.maximum(m_new, jnp.max(s, axis=1, keepdims=True))
        scale = jnp.exp2(m_old - m_new)
        l = scale * l_sc[...]
        acc = scale * acc_sc[...]
        for s, vals in zip(s_list, v_list):
            p = jnp.exp2(s - m_new)
            l = l + jnp.sum(p, axis=1, keepdims=True)
            acc = acc + jnp.dot(p.astype(BF16), vals, preferred_element_type=F32)
        l_sc[...] = l
        acc_sc[...] = acc
        m_sc[...] = m_new

    def scores(keys):
        return lax.dot_general(qs, keys, (((1,), (1,)), ((), ())), preferred_element_type=F32)

    q = q_ref[0].astype(F32)
    qs = jnp.concatenate([_split_subheads(q[:, h * V_DIM:(h + 1) * V_DIM]) for h in range(N_HEADS)],
                         axis=0).astype(BF16)
    bias = bias_ref[...]
    s_list, v_list = [], []
    for g0 in range(0, pages, group):
        keys = jnp.concatenate([r[...] for r in k_refs[g0:g0 + group]], axis=0).astype(BF16)
        v_list.append(jnp.concatenate([r[...] for r in v_refs[g0:g0 + group]], axis=0).astype(BF16))
        s_list.append(scores(keys) + bias)
    update(s_list, v_list)

    @pl.when(c == pl.num_programs(1) - 1)
    def _():
        lam = _lambda_value(lam_ref, lam_init)
        n_rows = n_new * N_HEADS
        pad = jnp.zeros((LANES - n_rows, V_DIM), F32)
        keys = jnp.concatenate([kn_ref[...], pad], axis=0).astype(BF16)
        vals = jnp.concatenate([vn_ref[...], pad], axis=0).astype(BF16)
        s = scores(keys)
        row = lax.broadcasted_iota(jnp.int32, s.shape, 0)
        col = lax.broadcasted_iota(jnp.int32, s.shape, 1)
        ok = ((col % N_HEADS == row // sub_rows) & (col // N_HEADS <= row % n_new) & (col < n_rows))
        update([jnp.where(ok, s, NEG_BIG)], [vals])
        acc = acc_sc[...]
        l = l_sc[...]
        for h in range(N_HEADS):
            r0 = h * sub_rows
            o_ref[0, :, h * V_DIM:(h + 1) * V_DIM] = _finish_heads(
                acc[r0:r0 + n_new], l[r0:r0 + n_new],
                acc[r0 + n_new:r0 + sub_rows], l[r0 + n_new:r0 + sub_rows], lam, g_ref[...], lam_init)


def _attention_sample(page_table, q_bf, k_new, v_new, cache_k, cache_v, layer, lam_vec, g,
                      lam_init, pages, group):
    batch, n_new, _ = q_bf.shape
    n_pages = page_table.shape[1]
    page_rows = PAGE_SIZE * N_HEADS
    q_rows = N_HEADS * 2 * n_new
    per_b = lambda r, w: pl.BlockSpec((1, r, w), lambda b, c, pt: (b, 0, 0))
    new_rows = pl.BlockSpec((None, n_new * N_HEADS, V_DIM), lambda b, c, pt: (layer, b, 0))
    same_head = (np.arange(group * page_rows)[None, :] % N_HEADS) == (np.arange(q_rows)[:, None] // (2 * n_new))
    bias = jnp.asarray(np.where(same_head, 0.0, NEG_BIG), F32)

    def page_spec(i):
        return pl.BlockSpec((None, None, page_rows, V_DIM),
                            lambda b, c, pt: (layer, pt[b, c * pages + i], 0, 0))

    grid_spec = pltpu.PrefetchScalarGridSpec(
        num_scalar_prefetch=1,
        grid=(batch, n_pages // pages),
        in_specs=[
            pl.BlockSpec((4, HEAD_DIM), lambda b, c, pt: (0, 0)),
            pl.BlockSpec((1, V_DIM), lambda b, c, pt: (0, 0)),
            pl.BlockSpec(bias.shape, lambda b, c, pt: (0, 0)),
            per_b(n_new, D_QK), new_rows, new_rows,
        ] + [page_spec(i) for i in range(pages)] * 2,
        out_specs=per_b(n_new, D_ATT),
        scratch_shapes=[
            pltpu.VMEM((q_rows, 1), F32),
            pltpu.VMEM((q_rows, 1), F32),
            pltpu.VMEM((q_rows, V_DIM), F32),
        ],
    )
    return pl.pallas_call(
        functools.partial(_attn_sample_kernel, pages=pages, group=group, n_new=n_new,
                          lam_init=lam_init),
        grid_spec=grid_spec,
        out_shape=jax.ShapeDtypeStruct((batch, n_new, D_ATT), F32),
        compiler_params=_cparams(("arbitrary", "arbitrary")),
        name="attention_sample",
    )(page_table, lam_vec, g, bias, q_bf, k_new, v_new, *([cache_k] * pages), *([cache_v] * pages))


def _sigmoid(x):
    return 0.5 + 0.5 * jnp.tanh(0.5 * x)


def _gelu_tanh(x):
    return 0.5 * x * (1.0 + jnp.tanh(math.sqrt(2.0 / math.pi) * (x + 0.044715 * (x * x * x))))


def _tail_kernel(x_ref, y_ref, gates_ref, o_ref, wg_ref, bg_ref, wps_ref, wpa_ref, wo_ref,
                 lng_ref, lnb_ref, out_ref):
    ys = jnp.concatenate([y_ref[cb] for cb in range(N_CH_BLOCKS)], axis=1)
    ys = _gelu_tanh(ys)
    glu = jnp.dot(ys.astype(BF16), wg_ref[...], preferred_element_type=F32) + bg_ref[...]
    ys = ys * _sigmoid(glu)
    gate_s = gates_ref[:, 0:D_SSM]
    ys = ys * (gate_s * _sigmoid(gate_s))
    gate_a = gates_ref[:, D_SSM:2 * D_SSM]
    ya = o_ref[...] * (gate_a * _sigmoid(gate_a))
    zg_s = gates_ref[:, 2 * D_SSM:2 * D_SSM + D_MODEL]
    zg_a = gates_ref[:, 2 * D_SSM + D_MODEL:]
    merged = (_sigmoid(zg_s) * jnp.dot(ys.astype(BF16), wps_ref[...], preferred_element_type=F32)
              + _sigmoid(zg_a) * jnp.dot(ya.astype(BF16), wpa_ref[...], preferred_element_type=F32))
    out = jnp.dot(merged.astype(BF16), wo_ref[...], preferred_element_type=F32)
    r = DEEPNORM_ALPHA * x_ref[...] + out
    mu = jnp.mean(r, axis=-1, keepdims=True)
    rc = r - mu
    var = jnp.mean(rc * rc, axis=-1, keepdims=True)
    out_ref[...] = rc * lax.rsqrt(var + LN_EPS) * lng_ref[...] + lnb_ref[...]


def _tail(x, y_blk, gates, o, w_glu, b_glu, w_ps, w_pa, w_o, ln_g, ln_b, tm):
    m = x.shape[0]
    row = lambda i: (i, 0)
    const = lambda shape: pl.BlockSpec(shape, lambda i: (0, 0), pipeline_mode=pl.Buffered(1))
    return pl.pallas_call(
        _tail_kernel,
        grid=(m // tm,),
        in_specs=[
            pl.BlockSpec((tm, D_MODEL), row),
            pl.BlockSpec((N_CH_BLOCKS, tm, LANES), lambda i: (0, i, 0)),
            pl.BlockSpec((tm, N_GATES), row),
            pl.BlockSpec((tm, D_ATT), row),
            const((D_SSM, D_SSM)), const((1, D_SSM)),
            const((D_SSM, D_MODEL)), const((D_ATT, D_MODEL)), const((D_MODEL, D_MODEL)),
            const((1, D_MODEL)), const((1, D_MODEL)),
        ],
        out_specs=pl.BlockSpec((tm, D_MODEL), row),
        out_shape=jax.ShapeDtypeStruct((m, D_MODEL), F32),
        compiler_params=_cparams(("arbitrary",)),
        name="merge_tail",
    )(x, y_blk, gates, o, w_glu, b_glu, w_ps, w_pa, w_o, ln_g, ln_b)


def _rope_tables(pos):
    half = ROT_DIM // 2
    inv = ROPE_THETA ** (-jnp.arange(0, ROT_DIM, 2, dtype=F32) / ROT_DIM)
    ang = pos.astype(F32)[:, None] * inv[None, :]
    cos, sin = jnp.cos(ang), jnp.sin(ang)
    n = pos.shape[0]
    pad = HEAD_DIM - ROT_DIM
    c = jnp.concatenate([cos, cos, jnp.ones((n, pad), F32)], axis=1)
    sa = jnp.concatenate([-sin, jnp.zeros((n, half + pad), F32)], axis=1)
    sb = jnp.concatenate([jnp.zeros((n, half), F32), sin, jnp.zeros((n, pad), F32)], axis=1)
    rep = LANES // HEAD_DIM
    return jnp.tile(c, (1, rep)), jnp.tile(sa, (1, rep)), jnp.tile(sb, (1, rep))


def kernel(x_prompt, x_sample, cache_k, cache_v, state_ssm_re, state_ssm_im, page_table, w_in, ssm_lambda_re, ssm_lambda_im, ssm_log_dt, ssm_b_re, ssm_b_im, ssm_c_re, ssm_c_im, ssm_d, w_glu, b_glu, lam_q1, lam_k1, lam_q2, lam_k2, subln_g, w_proj_s, w_proj_a, w_out, ln_g, ln_b):
    bsz_p, seq_p, _ = x_prompt.shape
    bsz_s, seq_s, _ = x_sample.shape
    assert bsz_p == 1 and seq_s == CHUNK
    n_pages = page_table.shape[1]
    past_len = n_pages * PAGE_SIZE
    m_p, m_s = bsz_p * seq_p, bsz_s * seq_s

    rope_p = _rope_tables(jnp.arange(seq_p))
    rope_s = tuple(jnp.tile(t, (bsz_s, 1)) for t in _rope_tables(past_len + jnp.arange(seq_s)))

    ck = cache_k.reshape(DEPTH, cache_k.shape[1], PAGE_SIZE * N_HEADS, V_DIM)
    cv = cache_v.reshape(DEPTH, cache_v.shape[1], PAGE_SIZE * N_HEADS, V_DIM)

    xp = x_prompt.reshape(m_p, D_MODEL)
    xs = x_sample.reshape(m_s, D_MODEL)
    zero_state = jnp.zeros((N_CH_BLOCKS, bsz_p, 2 * BLOCK_STATE), F32)
    kp_all = jnp.zeros((DEPTH, m_p * N_HEADS, V_DIM), F32)
    vp_all = jnp.zeros((DEPTH, m_p * N_HEADS, V_DIM), F32)
    ks_all = jnp.zeros((DEPTH, m_s * N_HEADS, V_DIM), F32)
    vs_all = jnp.zeros((DEPTH, m_s * N_HEADS, V_DIM), F32)
    outs = [[] for _ in range(4)]
    for l in range(DEPTH):
        lam_init = 0.8 - 0.6 * math.exp(-0.3 * l)
        w_bf = w_in[l].astype(BF16)
        ops = _s5_operators(ssm_lambda_re[l], ssm_lambda_im[l], ssm_log_dt[l], ssm_b_re[l],
                            ssm_b_im[l], ssm_c_re[l], ssm_c_im[l], ssm_d[l])
        lam_vec = jnp.stack([lam_q1[l], lam_k1[l], lam_q2[l], lam_k2[l]], axis=0)
        g = subln_g[l].reshape(1, V_DIM)
        tail_w = (w_glu[l].astype(BF16), b_glu[l].reshape(1, D_SSM), w_proj_s[l].astype(BF16),
                  w_proj_a[l].astype(BF16), w_out[l].astype(BF16),
                  ln_g[l].reshape(1, D_MODEL), ln_b[l].reshape(1, D_MODEL))

        u, gates, q_bf, kp_all, k_bf, vp_all, v_bf = _in_projection(
            xp, w_bf, *rope_p, kp_all, vp_all, l, tm=512)
        y, sfin = _s5_branch(u, *ops, zero_state, independent_rows=False, rows=256)
        o = _attention_prompt(q_bf, k_bf, v_bf, lam_vec, g, lam_init, tile=512, unroll=8)
        xp = _tail(xp, y, gates, o, *tail_w, tm=512)
        sr, si = _blocks_to_state(sfin)
        outs[0].append(sr)
        outs[1].append(si)

        u, gates, q_bf, ks_all, _, vs_all, _ = _in_projection(
            xs, w_bf, *rope_s, ks_all, vs_all, l, tm=m_s)
        s0 = _state_to_blocks(state_ssm_re[l], state_ssm_im[l])
        y, sfin = _s5_branch(u, *ops, s0, independent_rows=True, rows=bsz_s)
        o = _attention_sample(page_table, q_bf.reshape(bsz_s, seq_s, D_QK), ks_all, vs_all,
                              ck, cv, l, lam_vec, g, lam_init, pages=16, group=8)
        xs = _tail(xs, y, gates, o.reshape(m_s, D_ATT), *tail_w, tm=m_s)
        sr, si = _blocks_to_state(sfin)
        outs[2].append(sr)
        outs[3].append(si)

    kv_p = (DEPTH, bsz_p, seq_p, N_HEADS, V_DIM)
    kv_s = (DEPTH, bsz_s, seq_s, N_HEADS, V_DIM)
    return (xp.reshape(bsz_p, seq_p, D_MODEL), xs.reshape(bsz_s, seq_s, D_MODEL),
            kp_all.reshape(kv_p), vp_all.reshape(kv_p), jnp.stack(outs[0]), jnp.stack(outs[1]),
            ks_all.reshape(kv_s), vs_all.reshape(kv_s), jnp.stack(outs[2]), jnp.stack(outs[3]))
```

```python
import functools
import math

import jax
import jax.numpy as jnp
import numpy as np
from jax import lax
from jax.experimental import pallas as pl
from jax.experimental.pallas import tpu as pltpu

F32 = jnp.float32
BF16 = jnp.bfloat16

D_MODEL = 1024
DEPTH = 2
PAGE_SIZE = 128
D_SSM = D_MODEL // 2
GROUP_CH = 16
N_GROUPS = D_SSM // GROUP_CH
STATE_DIM = 64
N_HEADS = 4
HEAD_DIM = D_MODEL // (4 * N_HEADS)
V_DIM = 2 * HEAD_DIM
D_QK = N_HEADS * 2 * HEAD_DIM
D_ATT = N_HEADS * V_DIM
ROT_DIM = HEAD_DIM // 4
ROPE_THETA = 500000.0
DEEPNORM_ALPHA = (2 * DEPTH) ** 0.25
LN_EPS = 1e-5
D_IN = 2 * D_SSM + 2 * D_QK + 2 * D_ATT + 2 * D_MODEL

LANES = 128
SUBLANES = 8

CHUNK = SUBLANES
GROUPS_PER_BLOCK = LANES // GROUP_CH
N_CH_BLOCKS = D_SSM // LANES
BLOCK_STATE = GROUPS_PER_BLOCK * STATE_DIM
BLOCK_IO = CHUNK * LANES

SCAN_UNROLL = 8
NEG_BIG = -1e30
LOG2_E = 1.4426950408889634
VMEM_LIMIT = 56 * 1024 * 1024

_OFF_U = 0
_OFF_GS = _OFF_U + D_SSM
_OFF_Q = _OFF_GS + D_SSM
_OFF_K = _OFF_Q + D_QK
_OFF_V = _OFF_K + D_QK
_OFF_GA = _OFF_V + D_ATT
_OFF_ZS = _OFF_GA + D_ATT
_OFF_ZA = _OFF_ZS + D_MODEL
N_GATES = 2 * D_SSM + 2 * D_MODEL


def _cparams(semantics):
    return pltpu.CompilerParams(dimension_semantics=semantics, vmem_limit_bytes=VMEM_LIMIT)


def _inproj_kernel(x_ref, w_ref, cos_ref, sa_ref, sb_ref, kf_in, vf_in,
                   u_ref, gates_ref, q_ref, kf_ref, kb_ref, vf_ref, vb_ref):
    del kf_in, vf_in
    tm = x_ref.shape[0]
    xb = x_ref[...].astype(BF16)

    def seg(off, width):
        return jnp.dot(xb, w_ref[:, off:off + width], preferred_element_type=F32)

    u = seg(_OFF_U, D_SSM)
    for cb in range(N_CH_BLOCKS):
        u_ref[cb] = u[:, cb * LANES:(cb + 1) * LANES]
    gates_ref[:, 0:D_SSM] = seg(_OFF_GS, D_SSM)
    gates_ref[:, D_SSM:2 * D_SSM] = seg(_OFF_GA, D_ATT)
    gates_ref[:, 2 * D_SSM:2 * D_SSM + D_MODEL] = seg(_OFF_ZS, D_MODEL)
    gates_ref[:, 2 * D_SSM + D_MODEL:] = seg(_OFF_ZA, D_MODEL)

    reps = D_QK // LANES
    cos = jnp.concatenate([cos_ref[...]] * reps, axis=1)
    sa = jnp.concatenate([sa_ref[...]] * reps, axis=1)
    sb = jnp.concatenate([sb_ref[...]] * reps, axis=1)
    half = ROT_DIM // 2

    def rope(t):
        return (t * cos + pltpu.roll(t, D_QK - half, 1) * sa + pltpu.roll(t, half, 1) * sb)

    q = rope(seg(_OFF_Q, D_QK))
    q_ref[...] = (q * (HEAD_DIM ** -0.5 * LOG2_E)).astype(BF16)
    k = rope(seg(_OFF_K, D_QK))
    kb_ref[...] = k.astype(BF16)
    v = seg(_OFF_V, D_ATT)
    vb_ref[...] = v.astype(BF16)
    for h in range(N_HEADS):
        kf_ref[pl.ds(h, tm, stride=N_HEADS), :] = k[:, h * V_DIM:(h + 1) * V_DIM]
        vf_ref[pl.ds(h, tm, stride=N_HEADS), :] = v[:, h * V_DIM:(h + 1) * V_DIM]


def _in_projection(x, w_bf, cos_t, sa_t, sb_t, kf_all, vf_all, layer, tm):
    m = x.shape[0]
    row = lambda i: (i, 0)
    kv_rows = pl.BlockSpec((None, tm * N_HEADS, V_DIM), lambda i: (layer, i, 0))
    return pl.pallas_call(
        _inproj_kernel,
        grid=(m // tm,),
        in_specs=[
            pl.BlockSpec((tm, D_MODEL), row),
            pl.BlockSpec((D_MODEL, D_IN), lambda i: (0, 0), pipeline_mode=pl.Buffered(1)),
            pl.BlockSpec((tm, LANES), row),
            pl.BlockSpec((tm, LANES), row),
            pl.BlockSpec((tm, LANES), row),
            pl.BlockSpec(memory_space=pl.ANY),
            pl.BlockSpec(memory_space=pl.ANY),
        ],
        out_specs=[
            pl.BlockSpec((N_CH_BLOCKS, tm, LANES), lambda i: (0, i, 0)),
            pl.BlockSpec((tm, N_GATES), row),
            pl.BlockSpec((tm, D_QK), row),
            kv_rows,
            pl.BlockSpec((tm, D_QK), row),
            kv_rows,
            pl.BlockSpec((tm, D_ATT), row),
        ],
        out_shape=[
            jax.ShapeDtypeStruct((N_CH_BLOCKS, m, LANES), F32),
            jax.ShapeDtypeStruct((m, N_GATES), F32),
            jax.ShapeDtypeStruct((m, D_QK), BF16),
            jax.ShapeDtypeStruct(kf_all.shape, F32),
            jax.ShapeDtypeStruct((m, D_QK), BF16),
            jax.ShapeDtypeStruct(vf_all.shape, F32),
            jax.ShapeDtypeStruct((m, D_ATT), BF16),
        ],
        input_output_aliases={5: 3, 6: 5},
        compiler_params=_cparams(("arbitrary",)),
        name="in_projection",
    )(x, w_bf, cos_t, sa_t, sb_t, kf_all, vf_all)


def _s5_kernel(x_ref, w_in_ref, w_st_ref, w_loc_ref, a_ref, d_ref, s0_ref,
               y_ref, sfin_ref, *scratch, independent_rows):
    rows = x_ref.shape[0] // CHUNK
    x = jnp.concatenate([x_ref[pl.ds(t, rows, stride=CHUNK), :] for t in range(CHUNK)], axis=1)
    xb = x.astype(BF16)
    p = jnp.dot(xb, w_in_ref[0], preferred_element_type=F32)
    a_re = a_ref[0][:, :BLOCK_STATE]
    a_im = a_ref[0][:, BLOCK_STATE:]

    def step(sr, si, pr, pi):
        return a_re * sr - a_im * si + pr, a_re * si + a_im * sr + pi

    if independent_rows:
        s_prev = s0_ref[0]
        nr, ni = step(s_prev[:, :BLOCK_STATE], s_prev[:, BLOCK_STATE:],
                      p[:, :BLOCK_STATE], p[:, BLOCK_STATE:])
        sfin_ref[0] = jnp.concatenate([nr, ni], axis=1)
    else:
        state_sc, p_sc, prev_sc = scratch

        @pl.when(pl.program_id(1) == 0)
        def _():
            state_sc[...] = s0_ref[0]

        p_sc[...] = p

        def body(r, carry):
            sr, si = carry
            prev_sc[pl.ds(r, 1), :BLOCK_STATE] = sr
            prev_sc[pl.ds(r, 1), BLOCK_STATE:] = si
            return step(sr, si, p_sc[pl.ds(r, 1), :BLOCK_STATE], p_sc[pl.ds(r, 1), BLOCK_STATE:])

        sr, si = lax.fori_loop(0, rows, body,
                               (state_sc[:, :BLOCK_STATE], state_sc[:, BLOCK_STATE:]),
                               unroll=SCAN_UNROLL)
        state_sc[:, :BLOCK_STATE] = sr
        state_sc[:, BLOCK_STATE:] = si
        sfin_ref[0] = state_sc[...]
        s_prev = prev_sc[...]

    y = lax.dot_general(s_prev.astype(BF16), w_st_ref[0], (((1,), (1,)), ((), ())),
                        preferred_element_type=F32)
    y = y + jnp.dot(xb, w_loc_ref[0], preferred_element_type=F32)
    y = y + d_ref[0] * x
    for t in range(CHUNK):
        y_ref[pl.ds(t, rows, stride=CHUNK), :] = y[:, t * LANES:(t + 1) * LANES]


def _s5_branch(u_blk, w_in, w_st, w_loc, a_chunk, d_blk, s0, layer, independent_rows, rows):
    n_chunks = u_blk.shape[1] // CHUNK
    n_seq = s0.shape[1]
    io_spec = pl.BlockSpec((None, rows * CHUNK, LANES), lambda c, t: (c, t, 0))
    wspec = pl.BlockSpec((None, 1, BLOCK_IO, BLOCK_IO), lambda c, t: (layer, c, 0, 0))
    vspec = lambda w: pl.BlockSpec((None, 1, 1, w), lambda c, t: (layer, c, 0, 0))
    scratch = [] if independent_rows else [
        pltpu.VMEM((1, 2 * BLOCK_STATE), F32),
        pltpu.VMEM((rows, 2 * BLOCK_STATE), F32),
        pltpu.VMEM((rows, 2 * BLOCK_STATE), F32),
    ]
    return pl.pallas_call(
        functools.partial(_s5_kernel, independent_rows=independent_rows),
        grid=(N_CH_BLOCKS, n_chunks // rows),
        in_specs=[
            io_spec,
            wspec, wspec, wspec,
            vspec(2 * BLOCK_STATE), vspec(BLOCK_IO),
            pl.BlockSpec((1, n_seq, 2 * BLOCK_STATE), lambda c, t: (c, 0, 0)),
        ],
        out_specs=[
            io_spec,
            pl.BlockSpec((1, n_seq, 2 * BLOCK_STATE), lambda c, t: (c, 0, 0)),
        ],
        out_shape=[
            jax.ShapeDtypeStruct(u_blk.shape, F32),
            jax.ShapeDtypeStruct(s0.shape, F32),
        ],
        scratch_shapes=scratch,
        compiler_params=_cparams(("arbitrary", "arbitrary")),
        name="s5_branch",
    )(u_blk, w_in, w_st, w_loc, a_chunk, d_blk, s0)


def _s5_operators(lam_re, lam_im, log_dt, b_re, b_im, c_re, c_im, d):
    g, gp, nb = N_GROUPS, GROUPS_PER_BLOCK, N_CH_BLOCKS
    dt = jnp.exp(log_dt)[:, None]
    kk = jnp.arange(CHUNK + 1, dtype=F32)[:, None, None]
    mag = jnp.exp(kk * (lam_re * dt)[None])
    ang = kk * (lam_im * dt)[None]
    ak_re, ak_im = mag * jnp.cos(ang), mag * jnp.sin(ang)
    ab_re, ab_im = ak_re[1], ak_im[1]
    den = lam_re * lam_re + lam_im * lam_im
    nr = ab_re - 1.0
    z_re = (nr * lam_re + ab_im * lam_im) / den
    z_im = (ab_im * lam_re - nr * lam_im) / den
    bb_re = z_re[..., None] * b_re - z_im[..., None] * b_im
    bb_im = z_re[..., None] * b_im + z_im[..., None] * b_re

    def group_rows(a):
        a = a.reshape(CHUNK, 2, nb, LANES, STATE_DIM).transpose(2, 0, 3, 1, 4)
        a = jnp.broadcast_to(a.reshape(nb, BLOCK_IO, 2, 1, STATE_DIM),
                             (nb, BLOCK_IO, 2, gp, STATE_DIM)).reshape(nb, BLOCK_IO, 2 * BLOCK_STATE)
        row_g = (np.arange(BLOCK_IO) % LANES) // GROUP_CH
        col_g = (np.arange(2 * BLOCK_STATE) % BLOCK_STATE) // STATE_DIM
        return jnp.where(jnp.asarray(row_g[:, None] == col_g[None, :]), a, 0.0).astype(BF16)

    kr = (CHUNK - 1.0) - kk[:CHUNK]
    mag_r = jnp.exp(kr * (lam_re * dt)[None])
    ang_r = kr * (lam_im * dt)[None]
    pw_re = (mag_r * jnp.cos(ang_r))[..., None]
    pw_im = (mag_r * jnp.sin(ang_r))[..., None]
    m_re = pw_re * bb_re[None] - pw_im * bb_im[None]
    m_im = pw_re * bb_im[None] + pw_im * bb_re[None]
    w_in = group_rows(jnp.stack([m_re, m_im], axis=1).transpose(0, 1, 2, 4, 3))

    e_re = c_re[None] * ak_re[1:, :, None, :] - c_im[None] * ak_im[1:, :, None, :]
    e_im = c_re[None] * ak_im[1:, :, None, :] + c_im[None] * ak_re[1:, :, None, :]
    w_st = group_rows(jnp.stack([e_re, -e_im], axis=1))

    ca_re = c_re[None] * ak_re[:CHUNK, :, None, :] - c_im[None] * ak_im[:CHUNK, :, None, :]
    ca_im = c_re[None] * ak_im[:CHUNK, :, None, :] + c_im[None] * ak_re[:CHUNK, :, None, :]
    bt_re = bb_re.transpose(0, 2, 1)[None, :, :, None]
    bt_im = bb_im.transpose(0, 2, 1)[None, :, :, None]
    kern = jnp.sum(ca_re[:, :, None] * bt_re - ca_im[:, :, None] * bt_im, axis=-1)
    kern = jnp.broadcast_to(kern.reshape(CHUNK, nb, LANES, 1, GROUP_CH),
                            (CHUNK, nb, LANES, gp, GROUP_CH)).reshape(CHUNK, nb, LANES, LANES)
    lane_g = np.arange(LANES) // GROUP_CH
    kern = jnp.where(jnp.asarray(lane_g[:, None] == lane_g[None, :]), kern, 0.0).astype(BF16)
    kern = jnp.concatenate([kern, jnp.zeros((1, nb, LANES, LANES), BF16)], axis=0)
    lag = np.arange(CHUNK)[None, :] - np.arange(CHUNK)[:, None]
    w_loc = kern[np.where(lag >= 0, lag, CHUNK)]
    w_loc = w_loc.transpose(2, 0, 3, 1, 4).reshape(nb, BLOCK_IO, BLOCK_IO)

    a_chunk = jnp.stack([ak_re[CHUNK], ak_im[CHUNK]], axis=0)
    a_chunk = a_chunk.reshape(2, nb, BLOCK_STATE).transpose(1, 0, 2).reshape(nb, 1, 2 * BLOCK_STATE)
    d_blk = jnp.tile(d.reshape(nb, 1, LANES), (1, 1, CHUNK))
    return w_in, w_st, w_loc, a_chunk, d_blk


def _state_to_blocks(s_re, s_im):
    n = s_re.shape[0]
    s = jnp.stack([s_re, s_im], axis=1).reshape(n, 2, N_CH_BLOCKS, BLOCK_STATE)
    return s.transpose(2, 0, 1, 3).reshape(N_CH_BLOCKS, n, 2 * BLOCK_STATE)


def _blocks_to_state(s):
    n = s.shape[1]
    s = s.reshape(N_CH_BLOCKS, n, 2, BLOCK_STATE).transpose(2, 1, 0, 3)
    s = s.reshape(2, n, N_GROUPS, STATE_DIM)
    return s[0], s[1]


def _lambda_value(lam_ref, lam_init):
    lv = lam_ref[...]
    t1 = jnp.sum(lv[0:1] * lv[1:2], axis=1, keepdims=True)
    t2 = jnp.sum(lv[2:3] * lv[3:4], axis=1, keepdims=True)
    return jnp.exp(t1) - jnp.exp(t2) + lam_init


def _split_subheads(q):
    lane = lax.broadcasted_iota(jnp.int32, q.shape, 1)
    zero = jnp.zeros_like(q)
    return jnp.concatenate([jnp.where(lane < HEAD_DIM, q, zero),
                            jnp.where(lane >= HEAD_DIM, q, zero)], axis=0)


def _finish_heads(o1, l1, o2, l2, lam, g, lam_init):
    of = o1 / l1 - lam * (o2 / l2)
    of = of * lax.rsqrt(jnp.mean(of * of, axis=-1, keepdims=True) + LN_EPS)
    return of * g * (1.0 - lam_init)


def _attn_prompt_kernel(lam_ref, g_ref, q_ref, k_ref, v_ref, o_ref, m_sc, acc_sc, s_sc, *, tile,
                        unroll, lam_init):
    i = pl.program_id(1)
    qs = _split_subheads(q_ref[...])
    m_sc[...] = jnp.full(m_sc.shape, NEG_BIG, F32)
    acc_sc[...] = jnp.zeros(acc_sc.shape, F32)
    ones = jnp.ones((tile, LANES), BF16)

    def scores(j):
        kj = k_ref[pl.ds(j * tile, tile), :]
        return lax.dot_general(qs, kj, (((1,), (1,)), ((), ())), preferred_element_type=F32)

    s_sc[...] = scores(0)

    def block(j, s, masked, m_old, acc):
        vj = jnp.concatenate([v_ref[pl.ds(j * tile, tile), :], ones], axis=1)
        if masked:
            row = lax.broadcasted_iota(jnp.int32, s.shape, 0)
            col = lax.broadcasted_iota(jnp.int32, s.shape, 1)
            qpos = jnp.where(row >= tile, row - tile, row)
            s = jnp.where(col <= qpos, s, NEG_BIG)
        m_new = jnp.maximum(m_old, jnp.max(s, axis=1, keepdims=True))
        p = jnp.exp2((s - jnp.concatenate([m_new] * (tile // LANES), axis=1)).astype(BF16))
        scale = jnp.concatenate([jnp.exp2(m_old - m_new)] * (2 * V_DIM // LANES), axis=1)
        acc = scale * acc + jnp.dot(p, vj, preferred_element_type=F32)
        return m_new, acc

    def run(j0, n):
        m, acc, s = m_sc[...], acc_sc[...], s_sc[...]
        for t in range(n):
            s_next = scores(j0 + t + 1)
            m, acc = block(j0 + t, s, False, m, acc)
            s = s_next
        m_sc[...] = m
        acc_sc[...] = acc
        s_sc[...] = s

    def many(jj, c):
        run(unroll * jj, unroll)
        return c

    lax.fori_loop(0, i // unroll, many, 0)
    half = unroll // 2
    rem = i % unroll

    @pl.when(rem >= half)
    def _():
        run(i - rem, half)

    for r in range(half):
        @pl.when(rem % half == r)
        def _():
            m, acc, s = m_sc[...], acc_sc[...], s_sc[...]
            for t in range(r):
                s_next = scores(i - r + t + 1)
                m, acc = block(i - r + t, s, False, m, acc)
                s = s_next
            _, acc = block(i, s, True, m, acc)
            lam = _lambda_value(lam_ref, lam_init)
            o_ref[...] = _finish_heads(acc[:tile, :V_DIM], acc[:tile, V_DIM:V_DIM + 1],
                                       acc[tile:, :V_DIM], acc[tile:, V_DIM:V_DIM + 1],
                                       lam, g_ref[...], lam_init)


def _attention_prompt(q_bf, k_bf, v_bf, lam_vec, g, lam_init, tile, unroll):
    seq = q_bf.shape[0]
    head_all = pl.BlockSpec((seq, 2 * HEAD_DIM), lambda h, i: (0, h))
    return pl.pallas_call(
        functools.partial(_attn_prompt_kernel, tile=tile, unroll=unroll, lam_init=lam_init),
        grid=(N_HEADS, seq // tile),
        in_specs=[
            pl.BlockSpec((4, HEAD_DIM), lambda h, i: (0, 0)),
            pl.BlockSpec((1, V_DIM), lambda h, i: (0, 0)),
            pl.BlockSpec((tile, 2 * HEAD_DIM), lambda h, i: (i, h)),
            head_all, head_all,
        ],
        out_specs=pl.BlockSpec((tile, V_DIM), lambda h, i: (i, h)),
        out_shape=jax.ShapeDtypeStruct((seq, D_ATT), F32),
        scratch_shapes=[pltpu.VMEM((2 * tile, LANES), F32), pltpu.VMEM((2 * tile, 2 * V_DIM), F32),
                        pltpu.VMEM((2 * tile, tile), F32)],
        compiler_params=_cparams(("arbitrary", "arbitrary")),
        name="attention_prompt",
    )(lam_vec, g, q_bf, k_bf, v_bf)


def _attn_sample_kernel(pt_ref, lam_ref, g_ref, bias_ref, q_ref, kn_ref, vn_ref, *rest,
                        pages, group, n_new, lam_init):
    del pt_ref
    k_refs = rest[:pages]
    v_refs = rest[pages:2 * pages]
    o_ref, m_sc, l_sc, acc_sc = rest[2 * pages:]
    c = pl.program_id(1)
    sub_rows = 2 * n_new

    @pl.when(c == 0)
    def _():
        m_sc[...] = jnp.full(m_sc.shape, NEG_BIG, F32)
        l_sc[...] = jnp.zeros(l_sc.shape, F32)
        acc_sc[...] = jnp.zeros(acc_sc.shape, F32)

    def update(s_list, v_list):
        m_old = m_sc[...]
        m_new = m_old
        for s in s_list:
            m_new = jnp.maximum(m_new, jnp.max(s, axis=1, keepdims=True))
        scale = jnp.exp2(m_old - m_new)
        l = scale * l_sc[...]
        acc = scale * acc_sc[...]
        for s, vals in zip(s_list, v_list):
            p = jnp.exp2(s - m_new)
            l = l + jnp.sum(p, axis=1, keepdims=True)
            acc = acc + jnp.dot(p.astype(BF16), vals, preferred_element_type=F32)
        l_sc[...] = l
        acc_sc[...] = acc
        m_sc[...] = m_new

    def scores(keys):
        return lax.dot_general(qs, keys, (((1,), (1,)), ((), ())), preferred_element_type=F32)

    q = q_ref[0].astype(F32)
    qs = jnp.concatenate([_split_subheads(q[:, h * V_DIM:(h + 1) * V_DIM]) for h in range(N_HEADS)],
                         axis=0).astype(BF16)
    bias = bias_ref[...]
    s_list, v_list = [], []
    for g0 in range(0, pages, group):
        keys = jnp.concatenate([r[...] for r in k_refs[g0:g0 + group]], axis=0).astype(BF16)
        v_list.append(jnp.concatenate([r[...] for r in v_refs[g0:g0 + group]], axis=0).astype(BF16))
        s_list.append(scores(keys) + bias)
    update(s_list, v_list)

    @pl.when(c == pl.num_programs(1) - 1)
    def _():
        lam = _lambda_value(lam_ref, lam_init)
        n_rows = n_new * N_HEADS
        pad = jnp.zeros((LANES - n_rows, V_DIM), F32)
        keys = jnp.concatenate([kn_ref[...], pad], axis=0).astype(BF16)
        vals = jnp.concatenate([vn_ref[...], pad], axis=0).astype(BF16)
        s = scores(keys)
        row = lax.broadcasted_iota(jnp.int32, s.shape, 0)
        col = lax.broadcasted_iota(jnp.int32, s.shape, 1)
        ok = ((col % N_HEADS == row // sub_rows) & (col // N_HEADS <= row % n_new) & (col < n_rows))
        update([jnp.where(ok, s, NEG_BIG)], [vals])
        acc = acc_sc[...]
        l = l_sc[...]
        for h in range(N_HEADS):
            r0 = h * sub_rows
            o_ref[0, :, h * V_DIM:(h + 1) * V_DIM] = _finish_heads(
                acc[r0:r0 + n_new], l[r0:r0 + n_new],
                acc[r0 + n_new:r0 + sub_rows], l[r0 + n_new:r0 + sub_rows], lam, g_ref[...], lam_init)


def _attention_sample(page_table, q_bf, k_new, v_new, cache_k, cache_v, layer, lam_vec, g,
                      lam_init, pages, group):
    batch, n_new, _ = q_bf.shape
    n_pages = page_table.shape[1]
    assert n_pages % pages == 0 and pages % group == 0
    page_rows = PAGE_SIZE * N_HEADS
    q_rows = N_HEADS * 2 * n_new
    per_b = lambda r, w: pl.BlockSpec((1, r, w), lambda b, c, pt: (b, 0, 0))
    new_rows = pl.BlockSpec((None, n_new * N_HEADS, V_DIM), lambda b, c, pt: (layer, b, 0))
    same_head = (np.arange(group * page_rows)[None, :] % N_HEADS) == (np.arange(q_rows)[:, None] // (2 * n_new))
    bias = jnp.asarray(np.where(same_head, 0.0, NEG_BIG), F32)

    def page_spec(i):
        return pl.BlockSpec((None, None, page_rows, V_DIM),
                            lambda b, c, pt: (layer, pt[b, c * pages + i], 0, 0))

    grid_spec = pltpu.PrefetchScalarGridSpec(
        num_scalar_prefetch=1,
        grid=(batch, n_pages // pages),
        in_specs=[
            pl.BlockSpec((4, HEAD_DIM), lambda b, c, pt: (0, 0)),
            pl.BlockSpec((1, V_DIM), lambda b, c, pt: (0, 0)),
            pl.BlockSpec(bias.shape, lambda b, c, pt: (0, 0)),
            per_b(n_new, D_QK), new_rows, new_rows,
        ] + [page_spec(i) for i in range(pages)] * 2,
        out_specs=per_b(n_new, D_ATT),
        scratch_shapes=[
            pltpu.VMEM((q_rows, 1), F32),
            pltpu.VMEM((q_rows, 1), F32),
            pltpu.VMEM((q_rows, V_DIM), F32),
        ],
    )
    return pl.pallas_call(
        functools.partial(_attn_sample_kernel, pages=pages, group=group, n_new=n_new,
                          lam_init=lam_init),
        grid_spec=grid_spec,
        out_shape=jax.ShapeDtypeStruct((batch, n_new, D_ATT), F32),
        compiler_params=_cparams(("arbitrary", "arbitrary")),
        name="attention_sample",
    )(page_table, lam_vec, g, bias, q_bf, k_new, v_new, *([cache_k] * pages), *([cache_v] * pages))


def _sigmoid(x):
    return 0.5 + 0.5 * jnp.tanh(0.5 * x)


def _gelu_tanh(x):
    return 0.5 * x * (1.0 + jnp.tanh(math.sqrt(2.0 / math.pi) * (x + 0.044715 * (x * x * x))))


def _tail_kernel(x_ref, y_ref, gates_ref, o_ref, wg_ref, bg_ref, wps_ref, wpa_ref, wo_ref,
                 lng_ref, lnb_ref, out_ref):
    ys = jnp.concatenate([y_ref[cb] for cb in range(N_CH_BLOCKS)], axis=1)
    ys = _gelu_tanh(ys)
    glu = jnp.dot(ys.astype(BF16), wg_ref[...], preferred_element_type=F32) + bg_ref[...]
    ys = ys * _sigmoid(glu)
    gate_s = gates_ref[:, 0:D_SSM]
    ys = ys * (gate_s * _sigmoid(gate_s))
    gate_a = gates_ref[:, D_SSM:2 * D_SSM]
    ya = o_ref[...] * (gate_a * _sigmoid(gate_a))
    zg_s = gates_ref[:, 2 * D_SSM:2 * D_SSM + D_MODEL]
    zg_a = gates_ref[:, 2 * D_SSM + D_MODEL:]
    merged = (_sigmoid(zg_s) * jnp.dot(ys.astype(BF16), wps_ref[...], preferred_element_type=F32)
              + _sigmoid(zg_a) * jnp.dot(ya.astype(BF16), wpa_ref[...], preferred_element_type=F32))
    out = jnp.dot(merged.astype(BF16), wo_ref[...], preferred_element_type=F32)
    r = DEEPNORM_ALPHA * x_ref[...] + out
    mu = jnp.mean(r, axis=-1, keepdims=True)
    rc = r - mu
    var = jnp.mean(rc * rc, axis=-1, keepdims=True)
    out_ref[...] = rc * lax.rsqrt(var + LN_EPS) * lng_ref[...] + lnb_ref[...]


def _tail(x, y_blk, gates, o, w_glu, b_glu, w_ps, w_pa, w_o, ln_g, ln_b, tm):
    m = x.shape[0]
    row = lambda i: (i, 0)
    const = lambda shape: pl.BlockSpec(shape, lambda i: (0, 0), pipeline_mode=pl.Buffered(1))
    return pl.pallas_call(
        _tail_kernel,
        grid=(m // tm,),
        in_specs=[
            pl.BlockSpec((tm, D_MODEL), row),
            pl.BlockSpec((N_CH_BLOCKS, tm, LANES), lambda i: (0, i, 0)),
            pl.BlockSpec((tm, N_GATES), row),
            pl.BlockSpec((tm, D_ATT), row),
            const((D_SSM, D_SSM)), const((1, D_SSM)),
            const((D_SSM, D_MODEL)), const((D_ATT, D_MODEL)), const((D_MODEL, D_MODEL)),
            const((1, D_MODEL)), const((1, D_MODEL)),
        ],
        out_specs=pl.BlockSpec((tm, D_MODEL), row),
        out_shape=jax.ShapeDtypeStruct((m, D_MODEL), F32),
        compiler_params=_cparams(("arbitrary",)),
        name="merge_tail",
    )(x, y_blk, gates, o, w_glu, b_glu, w_ps, w_pa, w_o, ln_g, ln_b)


def _rope_tables(pos):
    half = ROT_DIM // 2
    inv = ROPE_THETA ** (-jnp.arange(0, ROT_DIM, 2, dtype=F32) / ROT_DIM)
    ang = pos.astype(F32)[:, None] * inv[None, :]
    cos, sin = jnp.cos(ang), jnp.sin(ang)
    n = pos.shape[0]
    pad = HEAD_DIM - ROT_DIM
    c = jnp.concatenate([cos, cos, jnp.ones((n, pad), F32)], axis=1)
    sa = jnp.concatenate([-sin, jnp.zeros((n, half + pad), F32)], axis=1)
    sb = jnp.concatenate([jnp.zeros((n, half), F32), sin, jnp.zeros((n, pad), F32)], axis=1)
    rep = LANES // HEAD_DIM
    return jnp.tile(c, (1, rep)), jnp.tile(sa, (1, rep)), jnp.tile(sb, (1, rep))


def kernel(x_prompt, x_sample, cache_k, cache_v, state_ssm_re, state_ssm_im, page_table, w_in, ssm_lambda_re, ssm_lambda_im, ssm_log_dt, ssm_b_re, ssm_b_im, ssm_c_re, ssm_c_im, ssm_d, w_glu, b_glu, lam_q1, lam_k1, lam_q2, lam_k2, subln_g, w_proj_s, w_proj_a, w_out, ln_g, ln_b):
    bsz_p, seq_p, _ = x_prompt.shape
    bsz_s, seq_s, _ = x_sample.shape
    assert bsz_p == 1 and seq_s == CHUNK
    n_pages = page_table.shape[1]
    past_len = n_pages * PAGE_SIZE
    m_p, m_s = bsz_p * seq_p, bsz_s * seq_s

    rope_p = _rope_tables(jnp.arange(seq_p))
    rope_s = tuple(jnp.tile(t, (bsz_s, 1)) for t in _rope_tables(past_len + jnp.arange(seq_s)))

    ck = cache_k.reshape(DEPTH, cache_k.shape[1], PAGE_SIZE * N_HEADS, V_DIM)
    cv = cache_v.reshape(DEPTH, cache_v.shape[1], PAGE_SIZE * N_HEADS, V_DIM)

    xp = x_prompt.reshape(m_p, D_MODEL)
    xs = x_sample.reshape(m_s, D_MODEL)
    zero_state = jnp.zeros((N_CH_BLOCKS, bsz_p, 2 * BLOCK_STATE), F32)
    kp_all = jnp.zeros((DEPTH, m_p * N_HEADS, V_DIM), F32)
    vp_all = jnp.zeros((DEPTH, m_p * N_HEADS, V_DIM), F32)
    ks_all = jnp.zeros((DEPTH, m_s * N_HEADS, V_DIM), F32)
    vs_all = jnp.zeros((DEPTH, m_s * N_HEADS, V_DIM), F32)
    ops = jax.vmap(_s5_operators)(ssm_lambda_re, ssm_lambda_im, ssm_log_dt, ssm_b_re, ssm_b_im,
                                  ssm_c_re, ssm_c_im, ssm_d)
    outs = [[] for _ in range(4)]
    for l in range(DEPTH):
        lam_init = 0.8 - 0.6 * math.exp(-0.3 * l)
        w_bf = w_in[l].astype(BF16)
        lam_vec = jnp.stack([lam_q1[l], lam_k1[l], lam_q2[l], lam_k2[l]], axis=0)
        g = subln_g[l].reshape(1, V_DIM)
        tail_w = (w_glu[l].astype(BF16), b_glu[l].reshape(1, D_SSM), w_proj_s[l].astype(BF16),
                  w_proj_a[l].astype(BF16), w_out[l].astype(BF16),
                  ln_g[l].reshape(1, D_MODEL), ln_b[l].reshape(1, D_MODEL))

        u, gates, q_bf, kp_all, k_bf, vp_all, v_bf = _in_projection(
            xp, w_bf, *rope_p, kp_all, vp_all, l, tm=512)
        y, sfin = _s5_branch(u, *ops, zero_state, l, independent_rows=False, rows=256)
        o = _attention_prompt(q_bf, k_bf, v_bf, lam_vec, g, lam_init, tile=512, unroll=8)
        xp = _tail(xp, y, gates, o, *tail_w, tm=512)
        sr, si = _blocks_to_state(sfin)
        outs[0].append(sr)
        outs[1].append(si)

        u, gates, q_bf, ks_all, _, vs_all, _ = _in_projection(
            xs, w_bf, *rope_s, ks_all, vs_all, l, tm=m_s)
        s0 = _state_to_blocks(state_ssm_re[l], state_ssm_im[l])
        y, sfin = _s5_branch(u, *ops, s0, l, independent_rows=True, rows=bsz_s)
        o = _attention_sample(page_table, q_bf.reshape(bsz_s, seq_s, D_QK), ks_all, vs_all,
                              ck, cv, l, lam_vec, g, lam_init, pages=32, group=8)
        xs = _tail(xs, y, gates, o.reshape(m_s, D_ATT), *tail_w, tm=m_s)
        sr, si = _blocks_to_state(sfin)
        outs[2].append(sr)
        outs[3].append(si)

    kv_p = (DEPTH, bsz_p, seq_p, N_HEADS, V_DIM)
    kv_s = (DEPTH, bsz_s, seq_s, N_HEADS, V_DIM)
    return (xp.reshape(bsz_p, seq_p, D_MODEL), xs.reshape(bsz_s, seq_s, D_MODEL),
            kp_all.reshape(kv_p), vp_all.reshape(kv_p), jnp.stack(outs[0]), jnp.stack(outs[1]),
            ks_all.reshape(kv_s), vs_all.reshape(kv_s), jnp.stack(outs[2]), jnp.stack(outs[3]))
```

```python
import functools
import math

import jax
import jax.numpy as jnp
import numpy as np
from jax import lax
from jax.experimental import pallas as pl
from jax.experimental.pallas import tpu as pltpu

F32 = jnp.float32
BF16 = jnp.bfloat16

D_MODEL = 1024
DEPTH = 2
PAGE_SIZE = 128
D_SSM = D_MODEL // 2
GROUP_CH = 16
N_GROUPS = D_SSM // GROUP_CH
STATE_DIM = 64
N_HEADS = 4
HEAD_DIM = D_MODEL // (4 * N_HEADS)
V_DIM = 2 * HEAD_DIM
D_QK = N_HEADS * 2 * HEAD_DIM
D_ATT = N_HEADS * V_DIM
ROT_DIM = HEAD_DIM // 4
ROPE_THETA = 500000.0
DEEPNORM_ALPHA = (2 * DEPTH) ** 0.25
LN_EPS = 1e-5
D_IN = 2 * D_SSM + 2 * D_QK + 2 * D_ATT + 2 * D_MODEL

LANES = 128
SUBLANES = 8

CHUNK = SUBLANES
GROUPS_PER_BLOCK = LANES // GROUP_CH
N_CH_BLOCKS = D_SSM // LANES
BLOCK_STATE = GROUPS_PER_BLOCK * STATE_DIM
BLOCK_IO = CHUNK * LANES

SCAN_UNROLL = 8
NEG_BIG = -1e30
LOG2_E = 1.4426950408889634
VMEM_LIMIT = 56 * 1024 * 1024

_OFF_U = 0
_OFF_GS = _OFF_U + D_SSM
_OFF_Q = _OFF_GS + D_SSM
_OFF_K = _OFF_Q + D_QK
_OFF_V = _OFF_K + D_QK
_OFF_GA = _OFF_V + D_ATT
_OFF_ZS = _OFF_GA + D_ATT
_OFF_ZA = _OFF_ZS + D_MODEL
N_GATES = 2 * D_SSM + 2 * D_MODEL


def _cparams(semantics):
    return pltpu.CompilerParams(dimension_semantics=semantics, vmem_limit_bytes=VMEM_LIMIT)


def _inproj_kernel(x_ref, w_ref, cos_ref, sa_ref, sb_ref, *refs):
    u_ref, gates_ref, q_ref, kf_ref, kb_ref, vf_ref, vb_ref = refs[-7:]
    tm = x_ref.shape[0]
    xb = x_ref[...].astype(BF16)

    def seg(off, width):
        return jnp.dot(xb, w_ref[:, off:off + width], preferred_element_type=F32)

    u = seg(_OFF_U, D_SSM)
    for cb in range(N_CH_BLOCKS):
        u_ref[cb] = u[:, cb * LANES:(cb + 1) * LANES]
    gates_ref[:, 0:D_SSM] = seg(_OFF_GS, D_SSM)
    gates_ref[:, D_SSM:2 * D_SSM] = seg(_OFF_GA, D_ATT)
    gates_ref[:, 2 * D_SSM:2 * D_SSM + D_MODEL] = seg(_OFF_ZS, D_MODEL)
    gates_ref[:, 2 * D_SSM + D_MODEL:] = seg(_OFF_ZA, D_MODEL)

    reps = D_QK // LANES
    cos = jnp.concatenate([cos_ref[...]] * reps, axis=1)
    sa = jnp.concatenate([sa_ref[...]] * reps, axis=1)
    sb = jnp.concatenate([sb_ref[...]] * reps, axis=1)
    half = ROT_DIM // 2

    def rope(t):
        return (t * cos + pltpu.roll(t, D_QK - half, 1) * sa + pltpu.roll(t, half, 1) * sb)

    q = rope(seg(_OFF_Q, D_QK))
    q_ref[...] = (q * (HEAD_DIM ** -0.5 * LOG2_E)).astype(BF16)
    k = rope(seg(_OFF_K, D_QK))
    kb_ref[...] = k.astype(BF16)
    v = seg(_OFF_V, D_ATT)
    vb_ref[...] = v.astype(BF16)
    for d in range(kf_ref.shape[0]):
        for h in range(N_HEADS):
            kf_ref[d, pl.ds(h, tm, stride=N_HEADS), :] = k[:, h * V_DIM:(h + 1) * V_DIM]
            vf_ref[d, pl.ds(h, tm, stride=N_HEADS), :] = v[:, h * V_DIM:(h + 1) * V_DIM]


def _in_projection(x, w_bf, cos_t, sa_t, sb_t, kv_all, layer, tm):
    m = x.shape[0]
    row = lambda i: (i, 0)
    kv_shape = jax.ShapeDtypeStruct((DEPTH, m * N_HEADS, V_DIM), F32)
    if kv_all is None:
        kv_rows = pl.BlockSpec((DEPTH, tm * N_HEADS, V_DIM), lambda i: (0, i, 0))
        kv_in, kv_specs, aliases = (), [], {}
    else:
        kv_rows = pl.BlockSpec((1, tm * N_HEADS, V_DIM), lambda i: (layer, i, 0))
        kv_in, kv_specs, aliases = tuple(kv_all), [pl.BlockSpec(memory_space=pl.ANY)] * 2, {5: 3, 6: 5}
    return pl.pallas_call(
        _inproj_kernel,
        grid=(m // tm,),
        in_specs=[
            pl.BlockSpec((tm, D_MODEL), row),
            pl.BlockSpec((D_MODEL, D_IN), lambda i: (0, 0), pipeline_mode=pl.Buffered(1)),
            pl.BlockSpec((tm, LANES), row),
            pl.BlockSpec((tm, LANES), row),
            pl.BlockSpec((tm, LANES), row),
        ] + kv_specs,
        out_specs=[
            pl.BlockSpec((N_CH_BLOCKS, tm, LANES), lambda i: (0, i, 0)),
            pl.BlockSpec((tm, N_GATES), row),
            pl.BlockSpec((tm, D_QK), row),
            kv_rows,
            pl.BlockSpec((tm, D_QK), row),
            kv_rows,
            pl.BlockSpec((tm, D_ATT), row),
        ],
        out_shape=[
            jax.ShapeDtypeStruct((N_CH_BLOCKS, m, LANES), F32),
            jax.ShapeDtypeStruct((m, N_GATES), F32),
            jax.ShapeDtypeStruct((m, D_QK), BF16),
            kv_shape,
            jax.ShapeDtypeStruct((m, D_QK), BF16),
            kv_shape,
            jax.ShapeDtypeStruct((m, D_ATT), BF16),
        ],
        input_output_aliases=aliases,
        compiler_params=_cparams(("arbitrary",)),
        name="in_projection",
    )(x, w_bf, cos_t, sa_t, sb_t, *kv_in)


def _s5_kernel(x_ref, w_in_ref, w_st_ref, w_loc_ref, a_ref, d_ref, s0_ref,
               y_ref, sfin_ref, *scratch, independent_rows):
    rows = x_ref.shape[0] // CHUNK
    x = jnp.concatenate([x_ref[pl.ds(t, rows, stride=CHUNK), :] for t in range(CHUNK)], axis=1)
    xb = x.astype(BF16)
    p = jnp.dot(xb, w_in_ref[0], preferred_element_type=F32)
    a_re = a_ref[0][:, :BLOCK_STATE]
    a_im = a_ref[0][:, BLOCK_STATE:]

    def step(sr, si, pr, pi):
        return a_re * sr - a_im * si + pr, a_re * si + a_im * sr + pi

    if independent_rows:
        s_prev = s0_ref[0]
        nr, ni = step(s_prev[:, :BLOCK_STATE], s_prev[:, BLOCK_STATE:],
                      p[:, :BLOCK_STATE], p[:, BLOCK_STATE:])
        sfin_ref[0] = jnp.concatenate([nr, ni], axis=1)
    else:
        state_sc, p_sc, prev_sc = scratch

        @pl.when(pl.program_id(1) == 0)
        def _():
            state_sc[...] = s0_ref[0]

        p_sc[...] = p

        def body(r, carry):
            sr, si = carry
            prev_sc[pl.ds(r, 1), :BLOCK_STATE] = sr
            prev_sc[pl.ds(r, 1), BLOCK_STATE:] = si
            return step(sr, si, p_sc[pl.ds(r, 1), :BLOCK_STATE], p_sc[pl.ds(r, 1), BLOCK_STATE:])

        sr, si = lax.fori_loop(0, rows, body,
                               (state_sc[:, :BLOCK_STATE], state_sc[:, BLOCK_STATE:]),
                               unroll=SCAN_UNROLL)
        state_sc[:, :BLOCK_STATE] = sr
        state_sc[:, BLOCK_STATE:] = si
        sfin_ref[0] = state_sc[...]
        s_prev = prev_sc[...]

    y = lax.dot_general(s_prev.astype(BF16), w_st_ref[0], (((1,), (1,)), ((), ())),
                        preferred_element_type=F32)
    y = y + jnp.dot(xb, w_loc_ref[0], preferred_element_type=F32)
    y = y + d_ref[0] * x
    for t in range(CHUNK):
        y_ref[pl.ds(t, rows, stride=CHUNK), :] = y[:, t * LANES:(t + 1) * LANES]


def _s5_branch(u_blk, w_in, w_st, w_loc, a_chunk, d_blk, s0, layer, independent_rows, rows):
    n_chunks = u_blk.shape[1] // CHUNK
    n_seq = s0.shape[1]
    io_spec = pl.BlockSpec((None, rows * CHUNK, LANES), lambda c, t: (c, t, 0))
    wspec = pl.BlockSpec((None, 1, BLOCK_IO, BLOCK_IO), lambda c, t: (layer, c, 0, 0))
    vspec = lambda w: pl.BlockSpec((None, 1, 1, w), lambda c, t: (layer, c, 0, 0))
    scratch = [] if independent_rows else [
        pltpu.VMEM((1, 2 * BLOCK_STATE), F32),
        pltpu.VMEM((rows, 2 * BLOCK_STATE), F32),
        pltpu.VMEM((rows, 2 * BLOCK_STATE), F32),
    ]
    return pl.pallas_call(
        functools.partial(_s5_kernel, independent_rows=independent_rows),
        grid=(N_CH_BLOCKS, n_chunks // rows),
        in_specs=[
            io_spec,
            wspec, wspec, wspec,
            vspec(2 * BLOCK_STATE), vspec(BLOCK_IO),
            pl.BlockSpec((1, n_seq, 2 * BLOCK_STATE), lambda c, t: (c, 0, 0)),
        ],
        out_specs=[
            io_spec,
            pl.BlockSpec((1, n_seq, 2 * BLOCK_STATE), lambda c, t: (c, 0, 0)),
        ],
        out_shape=[
            jax.ShapeDtypeStruct(u_blk.shape, F32),
            jax.ShapeDtypeStruct(s0.shape, F32),
        ],
        scratch_shapes=scratch,
        compiler_params=_cparams(("arbitrary", "arbitrary")),
        name="s5_branch",
    )(u_blk, w_in, w_st, w_loc, a_chunk, d_blk, s0)


def _s5_operators(lam_re, lam_im, log_dt, b_re, b_im, c_re, c_im, d):
    g, gp, nb = N_GROUPS, GROUPS_PER_BLOCK, N_CH_BLOCKS
    dt = jnp.exp(log_dt)[:, None]
    kk = jnp.arange(CHUNK + 1, dtype=F32)[:, None, None]
    mag = jnp.exp(kk * (lam_re * dt)[None])
    ang = kk * (lam_im * dt)[None]
    ak_re, ak_im = mag * jnp.cos(ang), mag * jnp.sin(ang)
    ab_re, ab_im = ak_re[1], ak_im[1]
    den = lam_re * lam_re + lam_im * lam_im
    nr = ab_re - 1.0
    z_re = (nr * lam_re + ab_im * lam_im) / den
    z_im = (ab_im * lam_re - nr * lam_im) / den
    bb_re = z_re[..., None] * b_re - z_im[..., None] * b_im
    bb_im = z_re[..., None] * b_im + z_im[..., None] * b_re

    def group_rows(a):
        a = a.reshape(CHUNK, 2, nb, LANES, STATE_DIM).transpose(2, 0, 3, 1, 4)
        a = jnp.broadcast_to(a.reshape(nb, BLOCK_IO, 2, 1, STATE_DIM),
                             (nb, BLOCK_IO, 2, gp, STATE_DIM)).reshape(nb, BLOCK_IO, 2 * BLOCK_STATE)
        row_g = (np.arange(BLOCK_IO) % LANES) // GROUP_CH
        col_g = (np.arange(2 * BLOCK_STATE) % BLOCK_STATE) // STATE_DIM
        return jnp.where(jnp.asarray(row_g[:, None] == col_g[None, :]), a, 0.0).astype(BF16)

    kr = (CHUNK - 1.0) - kk[:CHUNK]
    mag_r = jnp.exp(kr * (lam_re * dt)[None])
    ang_r = kr * (lam_im * dt)[None]
    pw_re = (mag_r * jnp.cos(ang_r))[..., None]
    pw_im = (mag_r * jnp.sin(ang_r))[..., None]
    m_re = pw_re * bb_re[None] - pw_im * bb_im[None]
    m_im = pw_re * bb_im[None] + pw_im * bb_re[None]
    w_in = group_rows(jnp.stack([m_re, m_im], axis=1).transpose(0, 1, 2, 4, 3))

    e_re = c_re[None] * ak_re[1:, :, None, :] - c_im[None] * ak_im[1:, :, None, :]
    e_im = c_re[None] * ak_im[1:, :, None, :] + c_im[None] * ak_re[1:, :, None, :]
    w_st = group_rows(jnp.stack([e_re, -e_im], axis=1))

    ca_re = c_re[None] * ak_re[:CHUNK, :, None, :] - c_im[None] * ak_im[:CHUNK, :, None, :]
    ca_im = c_re[None] * ak_im[:CHUNK, :, None, :] + c_im[None] * ak_re[:CHUNK, :, None, :]
    bt_re = bb_re.transpose(0, 2, 1)[None, :, :, None]
    bt_im = bb_im.transpose(0, 2, 1)[None, :, :, None]
    kern = jnp.sum(ca_re[:, :, None] * bt_re - ca_im[:, :, None] * bt_im, axis=-1)
    kern = jnp.broadcast_to(kern.reshape(CHUNK, nb, LANES, 1, GROUP_CH),
                            (CHUNK, nb, LANES, gp, GROUP_CH)).reshape(CHUNK, nb, LANES, LANES)
    lane_g = np.arange(LANES) // GROUP_CH
    kern = jnp.where(jnp.asarray(lane_g[:, None] == lane_g[None, :]), kern, 0.0).astype(BF16)
    kern = jnp.concatenate([kern, jnp.zeros((1, nb, LANES, LANES), BF16)], axis=0)
    lag = np.arange(CHUNK)[None, :] - np.arange(CHUNK)[:, None]
    w_loc = kern[np.where(lag >= 0, lag, CHUNK)]
    w_loc = w_loc.transpose(2, 0, 3, 1, 4).reshape(nb, BLOCK_IO, BLOCK_IO)

    a_chunk = jnp.stack([ak_re[CHUNK], ak_im[CHUNK]], axis=0)
    a_chunk = a_chunk.reshape(2, nb, BLOCK_STATE).transpose(1, 0, 2).reshape(nb, 1, 2 * BLOCK_STATE)
    d_blk = jnp.tile(d.reshape(nb, 1, LANES), (1, 1, CHUNK))
    return w_in, w_st, w_loc, a_chunk, d_blk


def _state_to_blocks(s_re, s_im):
    n = s_re.shape[0]
    s = jnp.stack([s_re, s_im], axis=1).reshape(n, 2, N_CH_BLOCKS, BLOCK_STATE)
    return s.transpose(2, 0, 1, 3).reshape(N_CH_BLOCKS, n, 2 * BLOCK_STATE)


def _blocks_to_state(s):
    n = s.shape[1]
    s = s.reshape(N_CH_BLOCKS, n, 2, BLOCK_STATE).transpose(2, 1, 0, 3)
    s = s.reshape(2, n, N_GROUPS, STATE_DIM)
    return s[0], s[1]


def _lambda_value(lam_ref, lam_init):
    lv = lam_ref[...]
    t1 = jnp.sum(lv[0:1] * lv[1:2], axis=1, keepdims=True)
    t2 = jnp.sum(lv[2:3] * lv[3:4], axis=1, keepdims=True)
    return jnp.exp(t1) - jnp.exp(t2) + lam_init


def _split_subheads(q):
    lane = lax.broadcasted_iota(jnp.int32, q.shape, 1)
    zero = jnp.zeros_like(q)
    return jnp.concatenate([jnp.where(lane < HEAD_DIM, q, zero),
                            jnp.where(lane >= HEAD_DIM, q, zero)], axis=0)


def _finish_heads(o1, l1, o2, l2, lam, g, lam_init):
    of = o1 / l1 - lam * (o2 / l2)
    of = of * lax.rsqrt(jnp.mean(of * of, axis=-1, keepdims=True) + LN_EPS)
    return of * g * (1.0 - lam_init)


def _attn_prompt_kernel(lam_ref, g_ref, q_ref, k_ref, v_ref, o_ref, m_sc, acc_sc, s_sc, *, tile,
                        unroll, lam_init):
    i = pl.program_id(1)
    qs = _split_subheads(q_ref[...])
    m_sc[...] = jnp.full(m_sc.shape, NEG_BIG, F32)
    acc_sc[...] = jnp.zeros(acc_sc.shape, F32)
    ones = jnp.ones((tile, LANES), BF16)

    def scores(j):
        kj = k_ref[pl.ds(j * tile, tile), :]
        return lax.dot_general(qs, kj, (((1,), (1,)), ((), ())), preferred_element_type=F32)

    s_sc[...] = scores(0)

    def block(j, s, masked, m_old, acc):
        vj = jnp.concatenate([v_ref[pl.ds(j * tile, tile), :], ones], axis=1)
        if masked:
            row = lax.broadcasted_iota(jnp.int32, s.shape, 0)
            col = lax.broadcasted_iota(jnp.int32, s.shape, 1)
            qpos = jnp.where(row >= tile, row - tile, row)
            s = jnp.where(col <= qpos, s, NEG_BIG)
        m_new = jnp.maximum(m_old, jnp.max(s, axis=1, keepdims=True))
        p = jnp.exp2((s - jnp.concatenate([m_new] * (tile // LANES), axis=1)).astype(BF16))
        scale = jnp.concatenate([jnp.exp2(m_old - m_new)] * (2 * V_DIM // LANES), axis=1)
        acc = scale * acc + jnp.dot(p, vj, preferred_element_type=F32)
        return m_new, acc

    def run(j0, n):
        m, acc, s = m_sc[...], acc_sc[...], s_sc[...]
        for t in range(n):
            s_next = scores(j0 + t + 1)
            m, acc = block(j0 + t, s, False, m, acc)
            s = s_next
        m_sc[...] = m
        acc_sc[...] = acc
        s_sc[...] = s

    def many(jj, c):
        run(unroll * jj, unroll)
        return c

    lax.fori_loop(0, i // unroll, many, 0)
    half = unroll // 2
    rem = i % unroll

    @pl.when(rem >= half)
    def _():
        run(i - rem, half)

    for r in range(half):
        @pl.when(rem % half == r)
        def _():
            m, acc, s = m_sc[...], acc_sc[...], s_sc[...]
            for t in range(r):
                s_next = scores(i - r + t + 1)
                m, acc = block(i - r + t, s, False, m, acc)
                s = s_next
            _, acc = block(i, s, True, m, acc)
            lam = _lambda_value(lam_ref, lam_init)
            o_ref[...] = _finish_heads(acc[:tile, :V_DIM], acc[:tile, V_DIM:V_DIM + 1],
                                       acc[tile:, :V_DIM], acc[tile:, V_DIM:V_DIM + 1],
                                       lam, g_ref[...], lam_init)


def _attention_prompt(q_bf, k_bf, v_bf, lam_vec, g, lam_init, tile, unroll):
    seq = q_bf.shape[0]
    head_all = pl.BlockSpec((seq, 2 * HEAD_DIM), lambda h, i: (0, h))
    return pl.pallas_call(
        functools.partial(_attn_prompt_kernel, tile=tile, unroll=unroll, lam_init=lam_init),
        grid=(N_HEADS, seq // tile),
        in_specs=[
            pl.BlockSpec((4, HEAD_DIM), lambda h, i: (0, 0)),
            pl.BlockSpec((1, V_DIM), lambda h, i: (0, 0)),
            pl.BlockSpec((tile, 2 * HEAD_DIM), lambda h, i: (i, h)),
            head_all, head_all,
        ],
        out_specs=pl.BlockSpec((tile, V_DIM), lambda h, i: (i, h)),
        out_shape=jax.ShapeDtypeStruct((seq, D_ATT), F32),
        scratch_shapes=[pltpu.VMEM((2 * tile, LANES), F32), pltpu.VMEM((2 * tile, 2 * V_DIM), F32),
                        pltpu.VMEM((2 * tile, tile), F32)],
        compiler_params=_cparams(("arbitrary", "arbitrary")),
        name="attention_prompt",
    )(lam_vec, g, q_bf, k_bf, v_bf)


def _attn_sample_kernel(pt_ref, lam_ref, g_ref, bias_ref, q_ref, kn_ref, vn_ref, *rest,
                        pages, group, n_new, lam_init):
    del pt_ref
    k_refs = rest[:pages]
    v_refs = rest[pages:2 * pages]
    o_ref, m_sc, l_sc, acc_sc = rest[2 * pages:]
    c = pl.program_id(1)
    sub_rows = 2 * n_new

    @pl.when(c == 0)
    def _():
        m_sc[...] = jnp.full(m_sc.shape, NEG_BIG, F32)
        l_sc[...] = jnp.zeros(l_sc.shape, F32)
        acc_sc[...] = jnp.zeros(acc_sc.shape, F32)

    def update(s_list, v_list):
        m_old = m_sc[...]
        m_new = m_old
        for s in s_list:
            m_new = jnp.maximum(m_new, jnp.max(s, axis=1, keepdims=True))
        scale = jnp.exp2(m_old - m_new)
        l = scale * l_sc[...]
        acc = scale * acc_sc[...]
        for s, vals in zip(s_list, v_list):
            p = jnp.exp2(s - m_new)
            l = l + jnp.sum(p, axis=1, keepdims=True)
            acc = acc + jnp.dot(p.astype(BF16), vals, preferred_element_type=F32)
        l_sc[...] = l
        acc_sc[...] = acc
        m_sc[...] = m_new

    def scores(keys):
        return lax.dot_general(qs, keys, (((1,), (1,)), ((), ())), preferred_element_type=F32)

    q = q_ref[0].astype(F32)
    qs = jnp.concatenate([_split_subheads(q[:, h * V_DIM:(h + 1) * V_DIM]) for h in range(N_HEADS)],
                         axis=0).astype(BF16)
    bias = bias_ref[...]
    s_list, v_list = [], []
    for g0 in range(0, pages, group):
        keys = jnp.concatenate([r[...] for r in k_refs[g0:g0 + group]], axis=0).astype(BF16)
        v_list.append(jnp.concatenate([r[...] for r in v_refs[g0:g0 + group]], axis=0).astype(BF16))
        s_list.append(scores(keys) + bias)
    update(s_list, v_list)

    @pl.when(c == pl.num_programs(1) - 1)
    def _():
        lam = _lambda_value(lam_ref, lam_init)
        n_rows = n_new * N_HEADS
        pad = jnp.zeros((LANES - n_rows, V_DIM), F32)
        keys = jnp.concatenate([kn_ref[...], pad], axis=0).astype(BF16)
        vals = jnp.concatenate([vn_ref[...], pad], axis=0).astype(BF16)
        s = scores(keys)
        row = lax.broadcasted_iota(jnp.int32, s.shape, 0)
        col = lax.broadcasted_iota(jnp.int32, s.shape, 1)
        ok = ((col % N_HEADS == row // sub_rows) & (col // N_HEADS <= row % n_new) & (col < n_rows))
        update([jnp.where(ok, s, NEG_BIG)], [vals])
        acc = acc_sc[...]
        l = l_sc[...]
        for h in range(N_HEADS):
            r0 = h * sub_rows
            o_ref[0, :, h * V_DIM:(h + 1) * V_DIM] = _finish_heads(
                acc[r0:r0 + n_new], l[r0:r0 + n_new],
                acc[r0 + n_new:r0 + sub_rows], l[r0 + n_new:r0 + sub_rows], lam, g_ref[...], lam_init)


def _attention_sample(page_table, q_bf, k_new, v_new, cache_k, cache_v, layer, lam_vec, g,
                      lam_init, pages, group):
    batch, n_new, _ = q_bf.shape
    n_pages = page_table.shape[1]
    assert n_pages % pages == 0 and pages % group == 0
    page_rows = PAGE_SIZE * N_HEADS
    q_rows = N_HEADS * 2 * n_new
    per_b = lambda r, w: pl.BlockSpec((1, r, w), lambda b, c, pt: (b, 0, 0))
    new_rows = pl.BlockSpec((None, n_new * N_HEADS, V_DIM), lambda b, c, pt: (layer, b, 0))
    same_head = (np.arange(group * page_rows)[None, :] % N_HEADS) == (np.arange(q_rows)[:, None] // (2 * n_new))
    bias = jnp.asarray(np.where(same_head, 0.0, NEG_BIG), F32)

    def page_spec(i):
        return pl.BlockSpec((None, None, page_rows, V_DIM),
                            lambda b, c, pt: (layer, pt[b, c * pages + i], 0, 0))

    grid_spec = pltpu.PrefetchScalarGridSpec(
        num_scalar_prefetch=1,
        grid=(batch, n_pages // pages),
        in_specs=[
            pl.BlockSpec((4, HEAD_DIM), lambda b, c, pt: (0, 0)),
            pl.BlockSpec((1, V_DIM), lambda b, c, pt: (0, 0)),
            pl.BlockSpec(bias.shape, lambda b, c, pt: (0, 0)),
            per_b(n_new, D_QK), new_rows, new_rows,
        ] + [page_spec(i) for i in range(pages)] * 2,
        out_specs=per_b(n_new, D_ATT),
        scratch_shapes=[
            pltpu.VMEM((q_rows, 1), F32),
            pltpu.VMEM((q_rows, 1), F32),
            pltpu.VMEM((q_rows, V_DIM), F32),
        ],
    )
    return pl.pallas_call(
        functools.partial(_attn_sample_kernel, pages=pages, group=group, n_new=n_new,
                          lam_init=lam_init),
        grid_spec=grid_spec,
        out_shape=jax.ShapeDtypeStruct((batch, n_new, D_ATT), F32),
        compiler_params=_cparams(("arbitrary", "arbitrary")),
        name="attention_sample",
    )(page_table, lam_vec, g, bias, q_bf, k_new, v_new, *([cache_k] * pages), *([cache_v] * pages))


def _sigmoid(x):
    return 0.5 + 0.5 * jnp.tanh(0.5 * x)


def _silu(x):
    h = 0.5 * x
    return h + h * jnp.tanh(h)


def _gelu_tanh(x):
    c = math.sqrt(2.0 / math.pi)
    h = 0.5 * x
    return h + h * jnp.tanh(x * (c + (c * 0.044715) * (x * x)))


def _tail_kernel(x_ref, y_ref, gates_ref, o_ref, wg_ref, bg_ref, wps_ref, wpa_ref, wo_ref,
                 lng_ref, lnb_ref, out_ref):
    ys = jnp.concatenate([y_ref[cb] for cb in range(N_CH_BLOCKS)], axis=1)
    ys = _gelu_tanh(ys)
    glu = jnp.dot(ys.astype(BF16), wg_ref[...], preferred_element_type=F32) + bg_ref[...]
    ys = ys * _sigmoid(glu)
    gate_s = gates_ref[:, 0:D_SSM]
    ys = ys * _silu(gate_s)
    gate_a = gates_ref[:, D_SSM:2 * D_SSM]
    ya = o_ref[...] * _silu(gate_a)
    zg_s = gates_ref[:, 2 * D_SSM:2 * D_SSM + D_MODEL]
    zg_a = gates_ref[:, 2 * D_SSM + D_MODEL:]
    merged = (_sigmoid(zg_s) * jnp.dot(ys.astype(BF16), wps_ref[...], preferred_element_type=F32)
              + _sigmoid(zg_a) * jnp.dot(ya.astype(BF16), wpa_ref[...], preferred_element_type=F32))
    out = jnp.dot(merged.astype(BF16), wo_ref[...], preferred_element_type=F32)
    r = DEEPNORM_ALPHA * x_ref[...] + out
    mu = jnp.mean(r, axis=-1, keepdims=True)
    rc = r - mu
    var = jnp.mean(rc * rc, axis=-1, keepdims=True)
    out_ref[...] = rc * lax.rsqrt(var + LN_EPS) * lng_ref[...] + lnb_ref[...]


def _tail(x, y_blk, gates, o, w_glu, b_glu, w_ps, w_pa, w_o, ln_g, ln_b, tm):
    m = x.shape[0]
    row = lambda i: (i, 0)
    const = lambda shape: pl.BlockSpec(shape, lambda i: (0, 0), pipeline_mode=pl.Buffered(1))
    return pl.pallas_call(
        _tail_kernel,
        grid=(m // tm,),
        in_specs=[
            pl.BlockSpec((tm, D_MODEL), row),
            pl.BlockSpec((N_CH_BLOCKS, tm, LANES), lambda i: (0, i, 0)),
            pl.BlockSpec((tm, N_GATES), row),
            pl.BlockSpec((tm, D_ATT), row),
            const((D_SSM, D_SSM)), const((1, D_SSM)),
            const((D_SSM, D_MODEL)), const((D_ATT, D_MODEL)), const((D_MODEL, D_MODEL)),
            const((1, D_MODEL)), const((1, D_MODEL)),
        ],
        out_specs=pl.BlockSpec((tm, D_MODEL), row),
        out_shape=jax.ShapeDtypeStruct((m, D_MODEL), F32),
        compiler_params=_cparams(("arbitrary",)),
        name="merge_tail",
    )(x, y_blk, gates, o, w_glu, b_glu, w_ps, w_pa, w_o, ln_g, ln_b)


def _rope_tables(pos):
    half = ROT_DIM // 2
    inv = ROPE_THETA ** (-np.arange(0, ROT_DIM, 2, dtype=np.float64) / ROT_DIM)
    ang = np.asarray(pos, np.float64)[:, None] * inv[None, :]
    cos, sin = np.cos(ang), np.sin(ang)
    n = ang.shape[0]
    pad = HEAD_DIM - ROT_DIM
    c = np.concatenate([cos, cos, np.ones((n, pad))], axis=1)
    sa = np.concatenate([-sin, np.zeros((n, half + pad))], axis=1)
    sb = np.concatenate([np.zeros((n, half)), sin, np.zeros((n, pad))], axis=1)
    rep = LANES // HEAD_DIM
    return tuple(np.tile(t, (1, rep)).astype(np.float32) for t in (c, sa, sb))


def kernel(x_prompt, x_sample, cache_k, cache_v, state_ssm_re, state_ssm_im, page_table, w_in, ssm_lambda_re, ssm_lambda_im, ssm_log_dt, ssm_b_re, ssm_b_im, ssm_c_re, ssm_c_im, ssm_d, w_glu, b_glu, lam_q1, lam_k1, lam_q2, lam_k2, subln_g, w_proj_s, w_proj_a, w_out, ln_g, ln_b):
    bsz_p, seq_p, _ = x_prompt.shape
    bsz_s, seq_s, _ = x_sample.shape
    assert bsz_p == 1 and seq_s == CHUNK
    n_pages = page_table.shape[1]
    past_len = n_pages * PAGE_SIZE
    m_p, m_s = bsz_p * seq_p, bsz_s * seq_s

    rope_p = tuple(jnp.asarray(t) for t in _rope_tables(np.arange(seq_p)))
    rope_s = tuple(jnp.asarray(np.tile(t, (bsz_s, 1)))
                   for t in _rope_tables(past_len + np.arange(seq_s)))

    ck = cache_k.reshape(DEPTH, cache_k.shape[1], PAGE_SIZE * N_HEADS, V_DIM)
    cv = cache_v.reshape(DEPTH, cache_v.shape[1], PAGE_SIZE * N_HEADS, V_DIM)

    xp = x_prompt.reshape(m_p, D_MODEL)
    xs = x_sample.reshape(m_s, D_MODEL)
    zero_state = jnp.zeros((N_CH_BLOCKS, bsz_p, 2 * BLOCK_STATE), F32)
    kv_p = kv_s = None
    ops = jax.vmap(_s5_operators)(ssm_lambda_re, ssm_lambda_im, ssm_log_dt, ssm_b_re, ssm_b_im,
                                  ssm_c_re, ssm_c_im, ssm_d)
    outs = [[] for _ in range(4)]
    for l in range(DEPTH):
        lam_init = 0.8 - 0.6 * math.exp(-0.3 * l)
        w_bf = w_in[l].astype(BF16)
        lam_vec = jnp.stack([lam_q1[l], lam_k1[l], lam_q2[l], lam_k2[l]], axis=0)
        g = subln_g[l].reshape(1, V_DIM)
        tail_w = (w_glu[l].astype(BF16), b_glu[l].reshape(1, D_SSM), w_proj_s[l].astype(BF16),
                  w_proj_a[l].astype(BF16), w_out[l].astype(BF16),
                  ln_g[l].reshape(1, D_MODEL), ln_b[l].reshape(1, D_MODEL))

        u, gates, q_bf, kp_all, k_bf, vp_all, v_bf = _in_projection(
            xp, w_bf, *rope_p, kv_p, l, tm=512)
        kv_p = (kp_all, vp_all)
        y, sfin = _s5_branch(u, *ops, zero_state, l, independent_rows=False, rows=256)
        o = _attention_prompt(q_bf, k_bf, v_bf, lam_vec, g, lam_init, tile=512, unroll=8)
        xp = _tail(xp, y, gates, o, *tail_w, tm=512)
        sr, si = _blocks_to_state(sfin)
        outs[0].append(sr)
        outs[1].append(si)

        u, gates, q_bf, ks_all, _, vs_all, _ = _in_projection(
            xs, w_bf, *rope_s, kv_s, l, tm=m_s)
        kv_s = (ks_all, vs_all)
        s0 = _state_to_blocks(state_ssm_re[l], state_ssm_im[l])
        y, sfin = _s5_branch(u, *ops, s0, l, independent_rows=True, rows=bsz_s)
        o = _attention_sample(page_table, q_bf.reshape(bsz_s, seq_s, D_QK), ks_all, vs_all,
                              ck, cv, l, lam_vec, g, lam_init, pages=32, group=8)
        xs = _tail(xs, y, gates, o.reshape(m_s, D_ATT), *tail_w, tm=m_s)
        sr, si = _blocks_to_state(sfin)
        outs[2].append(sr)
        outs[3].append(si)

    shape_p = (DEPTH, bsz_p, seq_p, N_HEADS, V_DIM)
    shape_s = (DEPTH, bsz_s, seq_s, N_HEADS, V_DIM)
    return (xp.reshape(bsz_p, seq_p, D_MODEL), xs.reshape(bsz_s, seq_s, D_MODEL),
            kp_all.reshape(shape_p), vp_all.reshape(shape_p),
            jnp.stack(outs[0]), jnp.stack(outs[1]),
            ks_all.reshape(shape_s), vs_all.reshape(shape_s),
            jnp.stack(outs[2]), jnp.stack(outs[3]))
```

```python
import functools
import math

import jax
import jax.numpy as jnp
import numpy as np
from jax import lax
from jax.experimental import pallas as pl
from jax.experimental.pallas import tpu as pltpu

F32 = jnp.float32
BF16 = jnp.bfloat16

D_MODEL = 1024
DEPTH = 2
PAGE_SIZE = 128
D_SSM = D_MODEL // 2
GROUP_CH = 16
N_GROUPS = D_SSM // GROUP_CH
STATE_DIM = 64
N_HEADS = 4
HEAD_DIM = D_MODEL // (4 * N_HEADS)
V_DIM = 2 * HEAD_DIM
D_QK = N_HEADS * 2 * HEAD_DIM
D_ATT = N_HEADS * V_DIM
ROT_DIM = HEAD_DIM // 4
ROPE_THETA = 500000.0
DEEPNORM_ALPHA = (2 * DEPTH) ** 0.25
LN_EPS = 1e-5
D_IN = 2 * D_SSM + 2 * D_QK + 2 * D_ATT + 2 * D_MODEL

LANES = 128
SUBLANES = 8

CHUNK = SUBLANES
GROUPS_PER_BLOCK = LANES // GROUP_CH
N_CH_BLOCKS = D_SSM // LANES
BLOCK_STATE = GROUPS_PER_BLOCK * STATE_DIM
BLOCK_IO = CHUNK * LANES

NEG_BIG = -1e30
LOG2_E = 1.4426950408889634
VMEM_LIMIT = 56 * 1024 * 1024

_OFF_U = 0
_OFF_GS = _OFF_U + D_SSM
_OFF_Q = _OFF_GS + D_SSM
_OFF_K = _OFF_Q + D_QK
_OFF_V = _OFF_K + D_QK
_OFF_GA = _OFF_V + D_ATT
_OFF_ZS = _OFF_GA + D_ATT
_OFF_ZA = _OFF_ZS + D_MODEL
N_GATES = 2 * D_SSM + 2 * D_MODEL


def _cparams(semantics):
    return pltpu.CompilerParams(dimension_semantics=semantics, vmem_limit_bytes=VMEM_LIMIT)


def _inproj_kernel(x_ref, w_ref, cos_ref, sa_ref, sb_ref, *refs):
    u_ref, gates_ref, q_ref, kf_ref, kb_ref, vf_ref, vb_ref = refs[-7:]
    tm = x_ref.shape[0]
    xb = x_ref[...].astype(BF16)

    def seg(off, width):
        return jnp.dot(xb, w_ref[:, off:off + width], preferred_element_type=F32)

    u = seg(_OFF_U, D_SSM)
    for cb in range(N_CH_BLOCKS):
        u_ref[cb] = u[:, cb * LANES:(cb + 1) * LANES]
    gates_ref[:, 0:D_SSM] = _silu(seg(_OFF_GS, D_SSM))
    gates_ref[:, D_SSM:2 * D_SSM] = _silu(seg(_OFF_GA, D_ATT))
    gates_ref[:, 2 * D_SSM:2 * D_SSM + D_MODEL] = _sigmoid(seg(_OFF_ZS, D_MODEL))
    gates_ref[:, 2 * D_SSM + D_MODEL:] = _sigmoid(seg(_OFF_ZA, D_MODEL))

    reps = D_QK // LANES
    cos = jnp.concatenate([cos_ref[...]] * reps, axis=1)
    sa = jnp.concatenate([sa_ref[...]] * reps, axis=1)
    sb = jnp.concatenate([sb_ref[...]] * reps, axis=1)
    half = ROT_DIM // 2

    def rope(t):
        return (t * cos + pltpu.roll(t, D_QK - half, 1) * sa + pltpu.roll(t, half, 1) * sb)

    q = rope(seg(_OFF_Q, D_QK))
    q_ref[...] = (q * (HEAD_DIM ** -0.5 * LOG2_E)).astype(BF16)
    k = rope(seg(_OFF_K, D_QK))
    kb_ref[...] = k.astype(BF16)
    v = seg(_OFF_V, D_ATT)
    vb_ref[...] = v.astype(BF16)
    for d in range(kf_ref.shape[0]):
        for h in range(N_HEADS):
            kf_ref[d, pl.ds(h, tm, stride=N_HEADS), :] = k[:, h * V_DIM:(h + 1) * V_DIM]
            vf_ref[d, pl.ds(h, tm, stride=N_HEADS), :] = v[:, h * V_DIM:(h + 1) * V_DIM]


def _in_projection(x, w_bf, cos_t, sa_t, sb_t, kv_all, layer, tm):
    m = x.shape[0]
    row = lambda i: (i, 0)
    kv_shape = jax.ShapeDtypeStruct((DEPTH, m * N_HEADS, V_DIM), F32)
    if kv_all is None:
        kv_rows = pl.BlockSpec((DEPTH, tm * N_HEADS, V_DIM), lambda i: (0, i, 0))
        kv_in, kv_specs, aliases = (), [], {}
    else:
        kv_rows = pl.BlockSpec((1, tm * N_HEADS, V_DIM), lambda i: (layer, i, 0))
        kv_in, kv_specs, aliases = tuple(kv_all), [pl.BlockSpec(memory_space=pl.ANY)] * 2, {5: 3, 6: 5}
    return pl.pallas_call(
        _inproj_kernel,
        grid=(m // tm,),
        in_specs=[
            pl.BlockSpec((tm, D_MODEL), row),
            pl.BlockSpec((D_MODEL, D_IN), lambda i: (0, 0), pipeline_mode=pl.Buffered(1)),
            pl.BlockSpec((tm, LANES), row),
            pl.BlockSpec((tm, LANES), row),
            pl.BlockSpec((tm, LANES), row),
        ] + kv_specs,
        out_specs=[
            pl.BlockSpec((N_CH_BLOCKS, tm, LANES), lambda i: (0, i, 0)),
            pl.BlockSpec((tm, N_GATES), row),
            pl.BlockSpec((tm, D_QK), row),
            kv_rows,
            pl.BlockSpec((tm, D_QK), row),
            kv_rows,
            pl.BlockSpec((tm, D_ATT), row),
        ],
        out_shape=[
            jax.ShapeDtypeStruct((N_CH_BLOCKS, m, LANES), F32),
            jax.ShapeDtypeStruct((m, N_GATES), F32),
            jax.ShapeDtypeStruct((m, D_QK), BF16),
            kv_shape,
            jax.ShapeDtypeStruct((m, D_QK), BF16),
            kv_shape,
            jax.ShapeDtypeStruct((m, D_ATT), BF16),
        ],
        input_output_aliases=aliases,
        compiler_params=_cparams(("arbitrary",)),
        name="in_projection",
    )(x, w_bf, cos_t, sa_t, sb_t, *kv_in)


def _s5_kernel(x_ref, w_in_ref, w_st_ref, w_loc_ref, a_ref, d_ref, s0_ref,
               y_ref, sfin_ref, *scratch, independent_rows):
    rows = x_ref.shape[0] // CHUNK
    x = jnp.concatenate([x_ref[pl.ds(t, rows, stride=CHUNK), :] for t in range(CHUNK)], axis=1)
    xb = x.astype(BF16)
    p = jnp.dot(xb, w_in_ref[0], preferred_element_type=F32)
    y_loc = jnp.dot(xb, w_loc_ref[0], preferred_element_type=F32)
    a_re = a_ref[0][:, :BLOCK_STATE]
    a_im = a_ref[0][:, BLOCK_STATE:]

    def step(sr, si, pr, pi):
        return a_re * sr - a_im * si + pr, a_re * si + a_im * sr + pi

    if independent_rows:
        s_prev = s0_ref[0]
        nr, ni = step(s_prev[:, :BLOCK_STATE], s_prev[:, BLOCK_STATE:],
                      p[:, :BLOCK_STATE], p[:, BLOCK_STATE:])
        sfin_ref[0] = jnp.concatenate([nr, ni], axis=1)
    else:
        state_sc, p_sc, prev_sc = scratch

        @pl.when(pl.program_id(1) == 0)
        def _():
            state_sc[...] = s0_ref[0]

        p_sc[...] = p

        def body(r, carry):
            sr, si = carry
            prev_sc[pl.ds(r, 1), :BLOCK_STATE] = sr
            prev_sc[pl.ds(r, 1), BLOCK_STATE:] = si
            return step(sr, si, p_sc[pl.ds(r, 1), :BLOCK_STATE], p_sc[pl.ds(r, 1), BLOCK_STATE:])

        sr, si = lax.fori_loop(0, rows, body,
                               (state_sc[:, :BLOCK_STATE], state_sc[:, BLOCK_STATE:]),
                               unroll=True)
        state_sc[:, :BLOCK_STATE] = sr
        state_sc[:, BLOCK_STATE:] = si
        sfin_ref[0] = state_sc[...]
        s_prev = prev_sc[...]

    y = lax.dot_general(s_prev.astype(BF16), w_st_ref[0], (((1,), (1,)), ((), ())),
                        preferred_element_type=F32)
    y = y + y_loc
    y = y + d_ref[0] * x
    for t in range(CHUNK):
        y_ref[pl.ds(t, rows, stride=CHUNK), :] = y[:, t * LANES:(t + 1) * LANES]


def _s5_branch(u_blk, w_in, w_st, w_loc, a_chunk, d_blk, s0, layer, independent_rows, rows):
    n_chunks = u_blk.shape[1] // CHUNK
    n_seq = s0.shape[1]
    io_spec = pl.BlockSpec((None, rows * CHUNK, LANES), lambda c, t: (c, t, 0))
    wspec = pl.BlockSpec((None, 1, BLOCK_IO, BLOCK_IO), lambda c, t: (layer, c, 0, 0))
    vspec = lambda w: pl.BlockSpec((None, 1, 1, w), lambda c, t: (layer, c, 0, 0))
    scratch = [] if independent_rows else [
        pltpu.VMEM((1, 2 * BLOCK_STATE), F32),
        pltpu.VMEM((rows, 2 * BLOCK_STATE), F32),
        pltpu.VMEM((rows, 2 * BLOCK_STATE), F32),
    ]
    return pl.pallas_call(
        functools.partial(_s5_kernel, independent_rows=independent_rows),
        grid=(N_CH_BLOCKS, n_chunks // rows),
        in_specs=[
            io_spec,
            wspec, wspec, wspec,
            vspec(2 * BLOCK_STATE), vspec(BLOCK_IO),
            pl.BlockSpec((1, n_seq, 2 * BLOCK_STATE), lambda c, t: (c, 0, 0)),
        ],
        out_specs=[
            io_spec,
            pl.BlockSpec((1, n_seq, 2 * BLOCK_STATE), lambda c, t: (c, 0, 0)),
        ],
        out_shape=[
            jax.ShapeDtypeStruct(u_blk.shape, F32),
            jax.ShapeDtypeStruct(s0.shape, F32),
        ],
        scratch_shapes=scratch,
        compiler_params=_cparams(("arbitrary", "arbitrary")),
        name="s5_branch",
    )(u_blk, w_in, w_st, w_loc, a_chunk, d_blk, s0)


def _s5_operators(lam_re, lam_im, log_dt, b_re, b_im, c_re, c_im, d):
    g, gp, nb = N_GROUPS, GROUPS_PER_BLOCK, N_CH_BLOCKS
    dt = jnp.exp(log_dt)[:, None]
    kk = jnp.arange(CHUNK + 1, dtype=F32)[:, None, None]
    mag = jnp.exp(kk * (lam_re * dt)[None])
    ang = kk * (lam_im * dt)[None]
    ak_re, ak_im = mag * jnp.cos(ang), mag * jnp.sin(ang)
    ab_re, ab_im = ak_re[1], ak_im[1]
    den = lam_re * lam_re + lam_im * lam_im
    nr = ab_re - 1.0
    z_re = (nr * lam_re + ab_im * lam_im) / den
    z_im = (ab_im * lam_re - nr * lam_im) / den
    bb_re = z_re[..., None] * b_re - z_im[..., None] * b_im
    bb_im = z_re[..., None] * b_im + z_im[..., None] * b_re

    def group_rows(a):
        a = a.reshape(CHUNK, 2, nb, LANES, STATE_DIM).transpose(2, 0, 3, 1, 4)
        a = jnp.broadcast_to(a.reshape(nb, BLOCK_IO, 2, 1, STATE_DIM),
                             (nb, BLOCK_IO, 2, gp, STATE_DIM)).reshape(nb, BLOCK_IO, 2 * BLOCK_STATE)
        row_g = (np.arange(BLOCK_IO) % LANES) // GROUP_CH
        col_g = (np.arange(2 * BLOCK_STATE) % BLOCK_STATE) // STATE_DIM
        return jnp.where(jnp.asarray(row_g[:, None] == col_g[None, :]), a, 0.0).astype(BF16)

    kr = (CHUNK - 1.0) - kk[:CHUNK]
    mag_r = jnp.exp(kr * (lam_re * dt)[None])
    ang_r = kr * (lam_im * dt)[None]
    pw_re = (mag_r * jnp.cos(ang_r))[..., None]
    pw_im = (mag_r * jnp.sin(ang_r))[..., None]
    m_re = pw_re * bb_re[None] - pw_im * bb_im[None]
    m_im = pw_re * bb_im[None] + pw_im * bb_re[None]
    w_in = group_rows(jnp.stack([m_re, m_im], axis=1).transpose(0, 1, 2, 4, 3))

    e_re = c_re[None] * ak_re[1:, :, None, :] - c_im[None] * ak_im[1:, :, None, :]
    e_im = c_re[None] * ak_im[1:, :, None, :] + c_im[None] * ak_re[1:, :, None, :]
    w_st = group_rows(jnp.stack([e_re, -e_im], axis=1))

    ca_re = c_re[None] * ak_re[:CHUNK, :, None, :] - c_im[None] * ak_im[:CHUNK, :, None, :]
    ca_im = c_re[None] * ak_im[:CHUNK, :, None, :] + c_im[None] * ak_re[:CHUNK, :, None, :]
    bt_re = bb_re.transpose(0, 2, 1)[None, :, :, None]
    bt_im = bb_im.transpose(0, 2, 1)[None, :, :, None]
    kern = jnp.sum(ca_re[:, :, None] * bt_re - ca_im[:, :, None] * bt_im, axis=-1)
    kern = jnp.broadcast_to(kern.reshape(CHUNK, nb, LANES, 1, GROUP_CH),
                            (CHUNK, nb, LANES, gp, GROUP_CH)).reshape(CHUNK, nb, LANES, LANES)
    lane_g = np.arange(LANES) // GROUP_CH
    kern = jnp.where(jnp.asarray(lane_g[:, None] == lane_g[None, :]), kern, 0.0).astype(BF16)
    kern = jnp.concatenate([kern, jnp.zeros((1, nb, LANES, LANES), BF16)], axis=0)
    lag = np.arange(CHUNK)[None, :] - np.arange(CHUNK)[:, None]
    w_loc = kern[np.where(lag >= 0, lag, CHUNK)]
    w_loc = w_loc.transpose(2, 0, 3, 1, 4).reshape(nb, BLOCK_IO, BLOCK_IO)

    a_chunk = jnp.stack([ak_re[CHUNK], ak_im[CHUNK]], axis=0)
    a_chunk = a_chunk.reshape(2, nb, BLOCK_STATE).transpose(1, 0, 2).reshape(nb, 1, 2 * BLOCK_STATE)
    d_blk = jnp.tile(d.reshape(nb, 1, LANES), (1, 1, CHUNK))
    return w_in, w_st, w_loc, a_chunk, d_blk


def _state_to_blocks(s_re, s_im):
    n = s_re.shape[0]
    s = jnp.stack([s_re, s_im], axis=1).reshape(n, 2, N_CH_BLOCKS, BLOCK_STATE)
    return s.transpose(2, 0, 1, 3).reshape(N_CH_BLOCKS, n, 2 * BLOCK_STATE)


def _blocks_to_state(s):
    n = s.shape[1]
    s = s.reshape(N_CH_BLOCKS, n, 2, BLOCK_STATE).transpose(2, 1, 0, 3)
    s = s.reshape(2, n, N_GROUPS, STATE_DIM)
    return s[0], s[1]


def _lambda_value(lam_ref, lam_init):
    lv = lam_ref[...]
    t1 = jnp.sum(lv[0:1] * lv[1:2], axis=1, keepdims=True)
    t2 = jnp.sum(lv[2:3] * lv[3:4], axis=1, keepdims=True)
    return jnp.exp(t1) - jnp.exp(t2) + lam_init


def _split_subheads(q):
    lane = lax.broadcasted_iota(jnp.int32, q.shape, 1)
    zero = jnp.zeros_like(q)
    return jnp.concatenate([jnp.where(lane < HEAD_DIM, q, zero),
                            jnp.where(lane >= HEAD_DIM, q, zero)], axis=0)


def _finish_heads(o1, l1, o2, l2, lam, g, lam_init):
    of = o1 / l1 - lam * (o2 / l2)
    of = of * lax.rsqrt(jnp.mean(of * of, axis=-1, keepdims=True) + LN_EPS)
    return of * g * (1.0 - lam_init)


def _attn_prompt_kernel(lam_ref, g_ref, q_ref, k_ref, v_ref, o_ref, m_sc, acc_sc, s_sc, *, tile,
                        unroll, lam_init):
    step, n_steps = pl.program_id(1), pl.num_programs(1)

    def one_tile(part, carry):
        _attn_prompt_tile(part, step + part * n_steps, lam_ref, g_ref, q_ref, k_ref, v_ref, o_ref,
                          m_sc, acc_sc, s_sc, tile=tile, unroll=unroll, lam_init=lam_init)
        return carry

    lax.fori_loop(0, 2, one_tile, 0)


def _attn_prompt_tile(part, i, lam_ref, g_ref, q_ref, k_ref, v_ref, o_ref, m_sc, acc_sc, s_sc, *,
                      tile, unroll, lam_init):
    qs = _split_subheads(q_ref[part])
    m_sc[...] = jnp.full(m_sc.shape, NEG_BIG, F32)
    acc_sc[...] = jnp.zeros(acc_sc.shape, F32)
    ones = jnp.ones((tile, LANES), BF16)

    def scores(j):
        kj = k_ref[pl.ds(j * tile, tile), :]
        return lax.dot_general(qs, kj, (((1,), (1,)), ((), ())), preferred_element_type=F32)

    s_sc[...] = scores(0)

    def block(j, s, masked, m_old, acc):
        vj = jnp.concatenate([v_ref[pl.ds(j * tile, tile), :], ones], axis=1)
        if masked:
            row = lax.broadcasted_iota(jnp.int32, s.shape, 0)
            col = lax.broadcasted_iota(jnp.int32, s.shape, 1)
            qpos = jnp.where(row >= tile, row - tile, row)
            s = jnp.where(col <= qpos, s, NEG_BIG)
        m_new = jnp.maximum(m_old, jnp.max(s, axis=1, keepdims=True))
        p = jnp.exp2((s - jnp.concatenate([m_new] * (tile // LANES), axis=1)).astype(BF16))
        scale = jnp.concatenate([jnp.exp2(m_old - m_new)] * (2 * V_DIM // LANES), axis=1)
        acc = scale * acc + jnp.dot(p, vj, preferred_element_type=F32)
        return m_new, acc

    def run(j0, n):
        m, acc, s = m_sc[...], acc_sc[...], s_sc[...]
        for t in range(n):
            s_next = scores(j0 + t + 1)
            m, acc = block(j0 + t, s, False, m, acc)
            s = s_next
        m_sc[...] = m
        acc_sc[...] = acc
        s_sc[...] = s

    def many(jj, c):
        run(unroll * jj, unroll)
        return c

    lax.fori_loop(0, i // unroll, many, 0)
    half = unroll // 2
    rem = i % unroll

    @pl.when(rem >= half)
    def _():
        run(i - rem, half)

    for r in range(half):
        @pl.when(rem % half == r)
        def _():
            m, acc, s = m_sc[...], acc_sc[...], s_sc[...]
            for t in range(r):
                s_next = scores(i - r + t + 1)
                m, acc = block(i - r + t, s, False, m, acc)
                s = s_next
            _, acc = block(i, s, True, m, acc)
            lam = _lambda_value(lam_ref, lam_init)
            o_ref[part] = _finish_heads(acc[:tile, :V_DIM], acc[:tile, V_DIM:V_DIM + 1],
                                        acc[tile:, :V_DIM], acc[tile:, V_DIM:V_DIM + 1],
                                        lam, g_ref[...], lam_init)


def _attention_prompt(q_bf, k_bf, v_bf, lam_vec, g, lam_init, tile, unroll):
    seq = q_bf.shape[0]
    assert seq % (2 * tile) == 0
    head_all = pl.BlockSpec((seq, 2 * HEAD_DIM), lambda h, i: (0, h))
    two_tiles = lambda w: pl.BlockSpec((2, tile, w), lambda h, i: (0, i, h))
    out = pl.pallas_call(
        functools.partial(_attn_prompt_kernel, tile=tile, unroll=unroll, lam_init=lam_init),
        grid=(N_HEADS, seq // (2 * tile)),
        in_specs=[
            pl.BlockSpec((4, HEAD_DIM), lambda h, i: (0, 0)),
            pl.BlockSpec((1, V_DIM), lambda h, i: (0, 0)),
            two_tiles(2 * HEAD_DIM),
            head_all, head_all,
        ],
        out_specs=two_tiles(V_DIM),
        out_shape=jax.ShapeDtypeStruct((2, seq // 2, D_ATT), F32),
        scratch_shapes=[pltpu.VMEM((2 * tile, LANES), F32), pltpu.VMEM((2 * tile, 2 * V_DIM), F32),
                        pltpu.VMEM((2 * tile, tile), F32)],
        compiler_params=_cparams(("arbitrary", "arbitrary")),
        name="attention_prompt",
    )(lam_vec, g, q_bf.reshape(2, seq // 2, D_QK), k_bf, v_bf)
    return out.reshape(seq, D_ATT)


def _attn_sample_kernel(pt_ref, lam_ref, g_ref, bias_ref, q_ref, kn_ref, vn_ref, *rest,
                        pages, group, n_new, lam_init):
    del pt_ref
    k_refs = rest[:pages]
    v_refs = rest[pages:2 * pages]
    o_ref, m_sc, l_sc, acc_sc = rest[2 * pages:]
    c = pl.program_id(1)
    sub_rows = 2 * n_new

    @pl.when(c == 0)
    def _():
        m_sc[...] = jnp.full(m_sc.shape, NEG_BIG, F32)
        l_sc[...] = jnp.zeros(l_sc.shape, F32)
        acc_sc[...] = jnp.zeros(acc_sc.shape, F32)

    def update(s_list, v_list):
        m_old = m_sc[...]
        m_new = m_old
        for s in s_list:
            m_new = jnp.maximum(m_new, jnp.max(s, axis=1, keepdims=True))
        scale = jnp.exp2(m_old - m_new)
        l = scale * l_sc[...]
        acc = scale * acc_sc[...]
        for s, vals in zip(s_list, v_list):
            p = jnp.exp2(s - m_new)
            l = l + jnp.sum(p, axis=1, keepdims=True)
            acc = acc + jnp.dot(p.astype(BF16), vals, preferred_element_type=F32)
        l_sc[...] = l
        acc_sc[...] = acc
        m_sc[...] = m_new

    def scores(keys):
        return lax.dot_general(qs, keys, (((1,), (1,)), ((), ())), preferred_element_type=F32)

    q = q_ref[0].astype(F32)
    qs = jnp.concatenate([_split_subheads(q[:, h * V_DIM:(h + 1) * V_DIM]) for h in range(N_HEADS)],
                         axis=0).astype(BF16)
    bias = bias_ref[...]
    s_list, v_list = [], []
    for g0 in range(0, pages, group):
        keys = jnp.concatenate([r[...] for r in k_refs[g0:g0 + group]], axis=0).astype(BF16)
        v_list.append(jnp.concatenate([r[...] for r in v_refs[g0:g0 + group]], axis=0).astype(BF16))
        s_list.append(scores(keys) + bias)
    update(s_list, v_list)

    @pl.when(c == pl.num_programs(1) - 1)
    def _():
        lam = _lambda_value(lam_ref, lam_init)
        n_rows = n_new * N_HEADS
        pad = jnp.zeros((LANES - n_rows, V_DIM), F32)
        keys = jnp.concatenate([kn_ref[...], pad], axis=0).astype(BF16)
        vals = jnp.concatenate([vn_ref[...], pad], axis=0).astype(BF16)
        s = scores(keys)
        row = lax.broadcasted_iota(jnp.int32, s.shape, 0)
        col = lax.broadcasted_iota(jnp.int32, s.shape, 1)
        ok = ((col % N_HEADS == row // sub_rows) & (col // N_HEADS <= row % n_new) & (col < n_rows))
        update([jnp.where(ok, s, NEG_BIG)], [vals])
        acc = acc_sc[...]
        l = l_sc[...]
        for h in range(N_HEADS):
            r0 = h * sub_rows
            o_ref[0, :, h * V_DIM:(h + 1) * V_DIM] = _finish_heads(
                acc[r0:r0 + n_new], l[r0:r0 + n_new],
                acc[r0 + n_new:r0 + sub_rows], l[r0 + n_new:r0 + sub_rows], lam, g_ref[...], lam_init)


def _attention_sample(page_table, q_bf, k_new, v_new, cache_k, cache_v, layer, lam_vec, g,
                      lam_init, pages, group):
    batch, n_new, _ = q_bf.shape
    n_pages = page_table.shape[1]
    assert n_pages % pages == 0 and pages % group == 0
    page_rows = PAGE_SIZE * N_HEADS
    q_rows = N_HEADS * 2 * n_new
    per_b = lambda r, w: pl.BlockSpec((1, r, w), lambda b, c, pt: (b, 0, 0))
    new_rows = pl.BlockSpec((None, n_new * N_HEADS, V_DIM), lambda b, c, pt: (layer, b, 0))
    same_head = (np.arange(group * page_rows)[None, :] % N_HEADS) == (np.arange(q_rows)[:, None] // (2 * n_new))
    bias = jnp.asarray(np.where(same_head, 0.0, NEG_BIG), F32)

    def page_spec(i):
        return pl.BlockSpec((None, None, page_rows, V_DIM),
                            lambda b, c, pt: (layer, pt[b, c * pages + i], 0, 0))

    grid_spec = pltpu.PrefetchScalarGridSpec(
        num_scalar_prefetch=1,
        grid=(batch, n_pages // pages),
        in_specs=[
            pl.BlockSpec((4, HEAD_DIM), lambda b, c, pt: (0, 0)),
            pl.BlockSpec((1, V_DIM), lambda b, c, pt: (0, 0)),
            pl.BlockSpec(bias.shape, lambda b, c, pt: (0, 0)),
            per_b(n_new, D_QK), new_rows, new_rows,
        ] + [page_spec(i) for i in range(pages)] * 2,
        out_specs=per_b(n_new, D_ATT),
        scratch_shapes=[
            pltpu.VMEM((q_rows, 1), F32),
            pltpu.VMEM((q_rows, 1), F32),
            pltpu.VMEM((q_rows, V_DIM), F32),
        ],
    )
    return pl.pallas_call(
        functools.partial(_attn_sample_kernel, pages=pages, group=group, n_new=n_new,
                          lam_init=lam_init),
        grid_spec=grid_spec,
        out_shape=jax.ShapeDtypeStruct((batch, n_new, D_ATT), F32),
        compiler_params=_cparams(("arbitrary", "arbitrary")),
        name="attention_sample",
    )(page_table, lam_vec, g, bias, q_bf, k_new, v_new, *([cache_k] * pages), *([cache_v] * pages))


def _sigmoid(x):
    return 0.5 + 0.5 * jnp.tanh(0.5 * x)


def _silu(x):
    h = 0.5 * x
    return h + h * jnp.tanh(h)


def _gelu_tanh(x):
    c = math.sqrt(2.0 / math.pi)
    h = 0.5 * x
    return h + h * jnp.tanh(x * (c + (c * 0.044715) * (x * x)))


def _tail_kernel(x_ref, y_ref, gates_ref, o_ref, wg_ref, bg_ref, wps_ref, wpa_ref, wo_ref,
                 lng_ref, lnb_ref, out_ref):
    ys = jnp.concatenate([y_ref[cb] for cb in range(N_CH_BLOCKS)], axis=1)
    ys = _gelu_tanh(ys)
    glu = jnp.dot(ys.astype(BF16), wg_ref[...], preferred_element_type=F32) + bg_ref[...]
    ys = ys * _sigmoid(glu)
    ys = ys * gates_ref[:, 0:D_SSM]
    ya = o_ref[...] * gates_ref[:, D_SSM:2 * D_SSM]
    mix_s = gates_ref[:, 2 * D_SSM:2 * D_SSM + D_MODEL]
    mix_a = gates_ref[:, 2 * D_SSM + D_MODEL:]
    merged = (mix_s * jnp.dot(ys.astype(BF16), wps_ref[...], preferred_element_type=F32)
              + mix_a * jnp.dot(ya.astype(BF16), wpa_ref[...], preferred_element_type=F32))
    out = jnp.dot(merged.astype(BF16), wo_ref[...], preferred_element_type=F32)
    r = DEEPNORM_ALPHA * x_ref[...] + out
    mu = jnp.mean(r, axis=-1, keepdims=True)
    rc = r - mu
    var = jnp.mean(rc * rc, axis=-1, keepdims=True)
    out_ref[...] = rc * lax.rsqrt(var + LN_EPS) * lng_ref[...] + lnb_ref[...]


def _tail(x, y_blk, gates, o, w_glu, b_glu, w_ps, w_pa, w_o, ln_g, ln_b, tm):
    m = x.shape[0]
    row = lambda i: (i, 0)
    const = lambda shape: pl.BlockSpec(shape, lambda i: (0, 0), pipeline_mode=pl.Buffered(1))
    return pl.pallas_call(
        _tail_kernel,
        grid=(m // tm,),
        in_specs=[
            pl.BlockSpec((tm, D_MODEL), row),
            pl.BlockSpec((N_CH_BLOCKS, tm, LANES), lambda i: (0, i, 0)),
            pl.BlockSpec((tm, N_GATES), row),
            pl.BlockSpec((tm, D_ATT), row),
            const((D_SSM, D_SSM)), const((1, D_SSM)),
            const((D_SSM, D_MODEL)), const((D_ATT, D_MODEL)), const((D_MODEL, D_MODEL)),
            const((1, D_MODEL)), const((1, D_MODEL)),
        ],
        out_specs=pl.BlockSpec((tm, D_MODEL), row),
        out_shape=jax.ShapeDtypeStruct((m, D_MODEL), F32),
        compiler_params=_cparams(("arbitrary",)),
        name="merge_tail",
    )(x, y_blk, gates, o, w_glu, b_glu, w_ps, w_pa, w_o, ln_g, ln_b)


def _rope_tables(pos):
    half = ROT_DIM // 2
    inv = ROPE_THETA ** (-np.arange(0, ROT_DIM, 2, dtype=np.float64) / ROT_DIM)
    ang = np.asarray(pos, np.float64)[:, None] * inv[None, :]
    cos, sin = np.cos(ang), np.sin(ang)
    n = ang.shape[0]
    pad = HEAD_DIM - ROT_DIM
    c = np.concatenate([cos, cos, np.ones((n, pad))], axis=1)
    sa = np.concatenate([-sin, np.zeros((n, half + pad))], axis=1)
    sb = np.concatenate([np.zeros((n, half)), sin, np.zeros((n, pad))], axis=1)
    rep = LANES // HEAD_DIM
    return tuple(np.tile(t, (1, rep)).astype(np.float32) for t in (c, sa, sb))


def kernel(x_prompt, x_sample, cache_k, cache_v, state_ssm_re, state_ssm_im, page_table, w_in, ssm_lambda_re, ssm_lambda_im, ssm_log_dt, ssm_b_re, ssm_b_im, ssm_c_re, ssm_c_im, ssm_d, w_glu, b_glu, lam_q1, lam_k1, lam_q2, lam_k2, subln_g, w_proj_s, w_proj_a, w_out, ln_g, ln_b):
    bsz_p, seq_p, _ = x_prompt.shape
    bsz_s, seq_s, _ = x_sample.shape
    assert bsz_p == 1 and seq_s == CHUNK
    n_pages = page_table.shape[1]
    past_len = n_pages * PAGE_SIZE
    m_p, m_s = bsz_p * seq_p, bsz_s * seq_s

    rope_p = tuple(jnp.asarray(t) for t in _rope_tables(np.arange(seq_p)))
    rope_s = tuple(jnp.asarray(np.tile(t, (bsz_s, 1)))
                   for t in _rope_tables(past_len + np.arange(seq_s)))

    ck = cache_k.reshape(DEPTH, cache_k.shape[1], PAGE_SIZE * N_HEADS, V_DIM)
    cv = cache_v.reshape(DEPTH, cache_v.shape[1], PAGE_SIZE * N_HEADS, V_DIM)

    xp = x_prompt.reshape(m_p, D_MODEL)
    xs = x_sample.reshape(m_s, D_MODEL)
    zero_state = jnp.zeros((N_CH_BLOCKS, bsz_p, 2 * BLOCK_STATE), F32)
    kv_p = kv_s = None
    ops = jax.vmap(_s5_operators)(ssm_lambda_re, ssm_lambda_im, ssm_log_dt, ssm_b_re, ssm_b_im,
                                  ssm_c_re, ssm_c_im, ssm_d)
    outs = [[] for _ in range(4)]
    for l in range(DEPTH):
        lam_init = 0.8 - 0.6 * math.exp(-0.3 * l)
        w_bf = w_in[l].astype(BF16)
        lam_vec = jnp.stack([lam_q1[l], lam_k1[l], lam_q2[l], lam_k2[l]], axis=0)
        g = subln_g[l].reshape(1, V_DIM)
        tail_w = (w_glu[l].astype(BF16), b_glu[l].reshape(1, D_SSM), w_proj_s[l].astype(BF16),
                  w_proj_a[l].astype(BF16), w_out[l].astype(BF16),
                  ln_g[l].reshape(1, D_MODEL), ln_b[l].reshape(1, D_MODEL))

        u, gates, q_bf, kp_all, k_bf, vp_all, v_bf = _in_projection(
            xp, w_bf, *rope_p, kv_p, l, tm=512)
        kv_p = (kp_all, vp_all)
        y, sfin = _s5_branch(u, *ops, zero_state, l, independent_rows=False, rows=256)
        o = _attention_prompt(q_bf, k_bf, v_bf, lam_vec, g, lam_init, tile=512, unroll=8)
        xp = _tail(xp, y, gates, o, *tail_w, tm=512)
        sr, si = _blocks_to_state(sfin)
        outs[0].append(sr)
        outs[1].append(si)

        u, gates, q_bf, ks_all, _, vs_all, _ = _in_projection(
            xs, w_bf, *rope_s, kv_s, l, tm=m_s)
        kv_s = (ks_all, vs_all)
        s0 = _state_to_blocks(state_ssm_re[l], state_ssm_im[l])
        y, sfin = _s5_branch(u, *ops, s0, l, independent_rows=True, rows=bsz_s)
        o = _attention_sample(page_table, q_bf.reshape(bsz_s, seq_s, D_QK), ks_all, vs_all,
                              ck, cv, l, lam_vec, g, lam_init, pages=32, group=8)
        xs = _tail(xs, y, gates, o.reshape(m_s, D_ATT), *tail_w, tm=m_s)
        sr, si = _blocks_to_state(sfin)
        outs[2].append(sr)
        outs[3].append(si)

    shape_p = (DEPTH, bsz_p, seq_p, N_HEADS, V_DIM)
    shape_s = (DEPTH, bsz_s, seq_s, N_HEADS, V_DIM)
    return (xp.reshape(bsz_p, seq_p, D_MODEL), xs.reshape(bsz_s, seq_s, D_MODEL),
            kp_all.reshape(shape_p), vp_all.reshape(shape_p),
            jnp.stack(outs[0]), jnp.stack(outs[1]),
            ks_all.reshape(shape_s), vs_all.reshape(shape_s),
            jnp.stack(outs[2]), jnp.stack(outs[3]))
```
